```python
import math
import jax, jax.numpy as jnp
from jax import lax
import numpy as np

D_MODEL = 2048
BATCH = 2
SEQ = 4096
DEPTH = 2

N_MIXERS = 2
N_POOL_LAYERS = (DEPTH + 1) // 2
N_ATTN_LAYERS = DEPTH // 2

POOL_WINDOWS = (2, 4, 8, 16)
N_POOL_GROUPS = len(POOL_WINDOWS)
POOL_GROUP = D_MODEL // N_POOL_GROUPS

DIFF_HEAD_DIM = 128
N_DIFF_HEADS = D_MODEL // (2 * DIFF_HEAD_DIM)
V_HEAD_DIM = 2 * DIFF_HEAD_DIM
Q_BLOCK = 128

D_FF = int(math.ceil((8 * D_MODEL / 3) / 256) * 256)

RMS_EPS = 1e-6

kernel_name = "hybrid_pool_diffattn_alibi_swiglu"


def rmsnorm(x, g, eps=RMS_EPS):
    xf = x.astype(jnp.float32)
    y = xf * lax.rsqrt(jnp.mean(xf * xf, axis=-1, keepdims=True) + eps)
    return (y * g.astype(jnp.float32)).astype(x.dtype)


def alibi_slopes(n_heads):
    return jnp.asarray([2.0 ** (-8.0 * (i + 1) / n_heads) for i in range(n_heads)], dtype=jnp.float32)


def causal_trailing_mean(h, w):
    S = h.shape[1]
    c = jnp.cumsum(h.astype(jnp.float32), axis=1)
    shifted = jnp.pad(c, ((0, 0), (w, 0), (0, 0)))[:, :S]
    cnt = jnp.minimum(jnp.arange(1, S + 1), w).astype(jnp.float32)
    return ((c - shifted) / cnt[None, :, None]).astype(h.dtype)


def pool_mixer(h, w_grp, scale):
    B, S, D = h.shape
    hg = h.reshape(B, S, N_POOL_GROUPS, POOL_GROUP)
    pooled = jnp.stack([causal_trailing_mean(hg[:, :, g], POOL_WINDOWS[g]) for g in range(N_POOL_GROUPS)], axis=2)
    y = jnp.einsum('bsgc,gcd->bsgd', pooled - hg, w_grp)
    return y.reshape(B, S, D) * scale


def diff_attention(h, w_qkv, lq1, lk1, lq2, lk2, subln_g, w_o, lambda_init):
    B, S, D = h.shape
    H, d, e = N_DIFF_HEADS, DIFF_HEAD_DIM, V_HEAD_DIM
    qkv = h @ w_qkv
    q = qkv[..., :D].reshape(B, S, H, 2, d)
    k = qkv[..., D:2 * D].reshape(B, S, H, 2, d)
    v = qkv[..., 2 * D:].reshape(B, S, H, e)
    lam = (jnp.exp(jnp.sum(lq1.astype(jnp.float32) * lk1.astype(jnp.float32)))
           - jnp.exp(jnp.sum(lq2.astype(jnp.float32) * lk2.astype(jnp.float32)))
           + lambda_init)
    slopes = alibi_slopes(H)
    scale = d ** -0.5
    nb = S // Q_BLOCK
    q_blocks = q.reshape(B, nb, Q_BLOCK, H, 2, d).transpose(1, 0, 2, 3, 4, 5)
    kpos = jnp.arange(S)

    def one_block(args):
        qb, bi = args
        qpos = bi * Q_BLOCK + jnp.arange(Q_BLOCK)
        s = jnp.einsum('bqhcd,bkhcd->bhcqk', qb, k, preferred_element_type=jnp.float32) * scale
        dist = (qpos[:, None] - kpos[None, :]).astype(jnp.float32)
        bias = -slopes[:, None, None] * dist[None]
        s = jnp.where((dist >= 0)[None, None, None], s + bias[None, :, None], -jnp.inf)
        p = jax.nn.softmax(s, axis=-1)
        a = p[:, :, 0] - lam * p[:, :, 1]
        return jnp.einsum('bhqk,bkhe->bqhe', a.astype(v.dtype), v)

    o = lax.map(one_block, (q_blocks, jnp.arange(nb)))
    o = o.transpose(1, 0, 2, 3, 4).reshape(B, S, H, e)
    o = rmsnorm(o, subln_g) * (1.0 - lambda_init)
    return o.reshape(B, S, H * e) @ w_o


def swiglu(h, wg, wu, wd):
    return (jax.nn.silu(h @ wg) * (h @ wu)) @ wd


def setup_inputs(seed: int = 0) -> dict:
    key = jax.random.key(seed)
    ks = jax.random.split(key, 20)
    f32 = jnp.float32
    D = D_MODEL

    def nrm(k, shape, fan_in):
        return jax.random.normal(k, shape, f32) * (fan_in ** -0.5)

    def gain(k, shape):
        return 1.0 + 0.02 * jax.random.normal(k, shape, f32)

    return {
        "x": jax.random.normal(ks[0], (BATCH, SEQ, D), f32),
        "norm_mix": gain(ks[1], (DEPTH, D)),
        "norm_ffn": gain(ks[2], (DEPTH, D)),
        "pool_w": nrm(ks[3], (N_POOL_LAYERS, N_POOL_GROUPS, POOL_GROUP, POOL_GROUP), POOL_GROUP),
        "pool_scale": gain(ks[4], (N_POOL_LAYERS, D)),
        "w_qkv": nrm(ks[5], (N_ATTN_LAYERS, D, 3 * D), D),
        "lambda_q1": 0.1 * jax.random.normal(ks[6], (N_ATTN_LAYERS, DIFF_HEAD_DIM), f32),
        "lambda_k1": 0.1 * jax.random.normal(ks[7], (N_ATTN_LAYERS, DIFF_HEAD_DIM), f32),
        "lambda_q2": 0.1 * jax.random.normal(ks[8], (N_ATTN_LAYERS, DIFF_HEAD_DIM), f32),
        "lambda_k2": 0.1 * jax.random.normal(ks[9], (N_ATTN_LAYERS, DIFF_HEAD_DIM), f32),
        "subln_g": gain(ks[10], (N_ATTN_LAYERS, V_HEAD_DIM)),
        "w_o": nrm(ks[11], (N_ATTN_LAYERS, D, D), D),
        "w_gate": nrm(ks[12], (DEPTH, D, D_FF), D),
        "w_up": nrm(ks[13], (DEPTH, D, D_FF), D),
        "w_down": nrm(ks[14], (DEPTH, D_FF, D), D_FF),
        "final_norm": gain(ks[15], (D,)),
    }


def reference(x, norm_mix, norm_ffn, pool_w, pool_scale, w_qkv, lambda_q1, lambda_k1,
              lambda_q2, lambda_k2, subln_g, w_o, w_gate, w_up, w_down, final_norm):
    h = x
    for i in range(DEPTH):
        j = i // N_MIXERS
        hn = rmsnorm(h, norm_mix[i])
        if i % N_MIXERS == 0:
            mix = pool_mixer(hn, pool_w[j], pool_scale[j])
        else:
            lambda_init = 0.8 - 0.6 * math.exp(-0.3 * i)
            mix = diff_attention(hn, w_qkv[j], lambda_q1[j], lambda_k1[j], lambda_q2[j],
                                 lambda_k2[j], subln_g[j], w_o[j], lambda_init)
        h = h + mix
        h = h + swiglu(rmsnorm(h, norm_ffn[i]), w_gate[i], w_up[i], w_down[i])
    return rmsnorm(h, final_norm)
```

```python
import functools
import math

import jax
import jax.numpy as jnp
from jax import lax
from jax.experimental import pallas as pl
from jax.experimental.pallas import tpu as pltpu

RMS_EPS = 1e-6
POOL_WINDOWS = (2, 4, 8, 16)
POOL_HALO = 16
DIFF_HEAD_DIM = 128
V_HEAD_DIM = 2 * DIFF_HEAD_DIM
VMEM_LIMIT_BYTES = 56 * 1024 * 1024

_f32 = jnp.float32
_bf16 = jnp.bfloat16


def _params(n_axes):
    return pltpu.CompilerParams(
        dimension_semantics=("arbitrary",) * n_axes,
        vmem_limit_bytes=VMEM_LIMIT_BYTES,
    )


def _rms(x, g):
    ms = jnp.mean(x * x, axis=-1, keepdims=True)
    return x * lax.rsqrt(ms + RMS_EPS) * g


def _pool_kernel(x_ref, halo_ref, gmix_ref, w_ref, scale_ref, gffn_ref, h_ref, xn_ref, *, ts, group):
    s = pl.program_id(1)
    x = x_ref[0]
    g = gmix_ref[...]
    hn = _rms(x, g)
    halo = jnp.where(s > 0, _rms(halo_ref[0], g), 0.0)
    ext = jnp.concatenate([halo, hn], axis=0)
    t = s * ts + lax.broadcasted_iota(jnp.int32, (ts, 1), 0)
    ssq = jnp.zeros((ts, 1), _f32)
    for gi, w in enumerate(POOL_WINDOWS):
        cols = slice(gi * group, (gi + 1) * group)
        acc = ext[:, cols]
        span = 1
        while span < w:
            acc = acc + pltpu.roll(acc, span, axis=0)
            span *= 2
        cnt = jnp.minimum(t + 1, w).astype(_f32)
        pooled = acc[POOL_HALO:, :] / cnt
        diff = (pooled - hn[:, cols]).astype(_bf16)
        y = jnp.dot(diff, w_ref[gi], preferred_element_type=_f32)
        hg = x[:, cols] + y * scale_ref[:, cols]
        h_ref[0, :, cols] = hg
        ssq = ssq + jnp.sum(hg * hg, axis=-1, keepdims=True)
    d_model = x.shape[-1]
    inv = lax.rsqrt(ssq / d_model + RMS_EPS)
    xn_ref[0] = (h_ref[0] * inv * gffn_ref[...]).astype(_bf16)


def _pool_layer(x, g_mix, pool_w, pool_scale, g_ffn, *, ts=512):
    B, S, D = x.shape
    G, C, _ = pool_w.shape
    halo_blocks = ts // POOL_HALO
    kern = functools.partial(_pool_kernel, ts=ts, group=C)
    return pl.pallas_call(
        kern,
        grid=(B, S // ts),
        in_specs=[
            pl.BlockSpec((1, ts, D), lambda b, s: (b, s, 0)),
            pl.BlockSpec((1, POOL_HALO, D), lambda b, s: (b, jnp.maximum(s * halo_blocks - 1, 0), 0)),
            pl.BlockSpec((1, D), lambda b, s: (0, 0)),
            pl.BlockSpec((G, C, C), lambda b, s: (0, 0, 0)),
            pl.BlockSpec((1, D), lambda b, s: (0, 0)),
            pl.BlockSpec((1, D), lambda b, s: (0, 0)),
        ],
        out_specs=[
            pl.BlockSpec((1, ts, D), lambda b, s: (b, s, 0)),
            pl.BlockSpec((1, ts, D), lambda b, s: (b, s, 0)),
        ],
        out_shape=[
            jax.ShapeDtypeStruct((B, S, D), _f32),
            jax.ShapeDtypeStruct((B, S, D), _bf16),
        ],
        compiler_params=_params(2),
        name="pool_mixer",
    )(x, x, g_mix.reshape(1, D), pool_w, pool_scale.reshape(1, D), g_ffn.reshape(1, D))


def _ffn_up_kernel(xn_ref, wg_ref, wu_ref, o_ref):
    xn = xn_ref[...]
    gate = jnp.dot(xn, wg_ref[...], preferred_element_type=_f32)
    up = jnp.dot(xn, wu_ref[...], preferred_element_type=_f32)
    o_ref[...] = (gate / (1.0 + jnp.exp(-gate)) * up).astype(o_ref.dtype)


def _ffn_up(xn, wg, wu, *, tm=2048, tn=512):
    T, D = xn.shape
    F = wg.shape[1]
    return pl.pallas_call(
        _ffn_up_kernel,
        grid=(T // tm, F // tn),
        in_specs=[
            pl.BlockSpec((tm, D), lambda i, j: (i, 0)),
            pl.BlockSpec((D, tn), lambda i, j: (0, j)),
            pl.BlockSpec((D, tn), lambda i, j: (0, j)),
        ],
        out_specs=pl.BlockSpec((tm, tn), lambda i, j: (i, j)),
        out_shape=jax.ShapeDtypeStruct((T, F), _bf16),
        compiler_params=_params(2),
        name="ffn_up",
    )(xn, wg, wu)


def _ffn_down_kernel(hm_ref, wd_ref, res_ref, o_ref):
    o_ref[...] = res_ref[...] + jnp.dot(hm_ref[...], wd_ref[...], preferred_element_type=_f32)


def _ffn_down(hmid, wd, res, *, tm=1024, tn=512):
    T, F = hmid.shape
    D = wd.shape[1]
    return pl.pallas_call(
        _ffn_down_kernel,
        grid=(T // tm, D // tn),
        in_specs=[
            pl.BlockSpec((tm, F), lambda i, j: (i, 0)),
            pl.BlockSpec((F, tn), lambda i, j: (0, j)),
            pl.BlockSpec((tm, tn), lambda i, j: (i, j)),
        ],
        out_specs=pl.BlockSpec((tm, tn), lambda i, j: (i, j)),
        out_shape=jax.ShapeDtypeStruct((T, D), _f32),
        compiler_params=_params(2),
        name="ffn_down",
    )(hmid, wd, res)


def _qkv_kernel(h_ref, g_ref, w_ref, cs_ref, o_ref, xn_ref):
    @pl.when(pl.program_id(1) == 0)
    def _():
        xn_ref[...] = _rms(h_ref[...], g_ref[...]).astype(_bf16)

    acc = jnp.dot(xn_ref[...], w_ref[...], preferred_element_type=_f32)
    o_ref[...] = (acc * cs_ref[...]).astype(o_ref.dtype)


def _qkv_proj(h, g, w, colscale, *, tm=1024, tn=1536):
    T, D = h.shape
    N = w.shape[1]
    return pl.pallas_call(
        _qkv_kernel,
        grid=(T // tm, N // tn),
        in_specs=[
            pl.BlockSpec((tm, D), lambda i, j: (i, 0)),
            pl.BlockSpec((1, D), lambda i, j: (0, 0)),
            pl.BlockSpec((D, tn), lambda i, j: (0, j)),
            pl.BlockSpec((1, tn), lambda i, j: (0, j)),
        ],
        out_specs=pl.BlockSpec((tm, tn), lambda i, j: (i, j)),
        out_shape=jax.ShapeDtypeStruct((T, N), _bf16),
        scratch_shapes=[pltpu.VMEM((tm, D), _bf16)],
        compiler_params=_params(2),
        name="qkv_proj",
    )(h, g.reshape(1, D), w, colscale)


def _attn_kernel(slopes_ref, q_ref, k_ref, v_ref, lq1_ref, lk1_ref, lq2_ref, lk2_ref, sg_ref, o_ref,
                 bias_ref, bias_diag_ref, acc_ref, m_ref, l_ref, *, tq, lambda_init):
    d = DIFF_HEAD_DIM
    S = q_ref.shape[0]
    nq = S // tq
    slope = slopes_ref[pl.program_id(1)]

    row = lax.broadcasted_iota(jnp.int32, (tq, tq), 0)
    col = lax.broadcasted_iota(jnp.int32, (tq, tq), 1)
    local = -slope * (row - col).astype(_f32)
    bias_ref[...] = local
    bias_diag_ref[...] = jnp.where(col <= row, local, -jnp.inf)

    lam = (jnp.exp(jnp.sum(lq1_ref[...] * lk1_ref[...], keepdims=True))
           - jnp.exp(jnp.sum(lq2_ref[...] * lk2_ref[...], keepdims=True))
           + lambda_init)

    def tile_update(qi, kj, bias_tile_ref):
        c = -slope * ((qi - kj) * tq).astype(_f32)
        ks = pl.ds(pl.multiple_of(kj * tq, tq), tq)
        qs = pl.ds(pl.multiple_of(qi * tq, tq), tq)
        v = v_ref[ks, :]
        for comp in range(2):
            cs = slice(comp * d, (comp + 1) * d)
            q = q_ref[qs, cs]
            k = k_ref[ks, cs]
            s = lax.dot_general(q, k, (((1,), (1,)), ((), ())), preferred_element_type=_f32)
            s = s + bias_tile_ref[...]
            m_old = m_ref[comp]
            m_new = jnp.maximum(m_old, jnp.max(s, axis=-1, keepdims=True) + c)
            p = jnp.exp(s - (m_new - c))
            alpha = jnp.exp(m_old - m_new)
            l_ref[comp] = alpha * l_ref[comp] + jnp.sum(p, axis=-1, keepdims=True)
            m_ref[comp] = m_new
            acc_ref[comp] = alpha * acc_ref[comp] + jnp.dot(p.astype(_bf16), v, preferred_element_type=_f32)

    def q_body(qi, carry):
        m_ref[...] = jnp.full(m_ref.shape, -jnp.inf, _f32)
        l_ref[...] = jnp.zeros(l_ref.shape, _f32)
        acc_ref[...] = jnp.zeros(acc_ref.shape, _f32)

        def kv_body(kj, c2):
            tile_update(qi, kj, bias_ref)
            return c2

        lax.fori_loop(0, qi, kv_body, 0)
        tile_update(qi, qi, bias_diag_ref)

        o = acc_ref[0] / l_ref[0] - lam * (acc_ref[1] / l_ref[1])
        ms = jnp.mean(o * o, axis=-1, keepdims=True)
        on = o * lax.rsqrt(ms + RMS_EPS) * sg_ref[...] * (1.0 - lambda_init)
        o_ref[pl.ds(pl.multiple_of(qi * tq, tq), tq), :] = on.astype(o_ref.dtype)
        return carry

    lax.fori_loop(0, nq, q_body, 0)


def _diff_attention(qkv, slopes, lq1, lk1, lq2, lk2, subln_g, *, B, S, H, lambda_init, tq=512):
    d, e = DIFF_HEAD_DIM, V_HEAD_DIM
    kern = functools.partial(_attn_kernel, tq=tq, lambda_init=lambda_init)
    vec = lambda a: a.reshape(1, -1)
    return pl.pallas_call(
        kern,
        grid_spec=pltpu.PrefetchScalarGridSpec(
            num_scalar_prefetch=1,
            grid=(B, H),
            in_specs=[
                pl.BlockSpec((S, e), lambda b, h, sl: (b, h)),
                pl.BlockSpec((S, e), lambda b, h, sl: (b, H + h)),
                pl.BlockSpec((S, e), lambda b, h, sl: (b, 2 * H + h)),
                pl.BlockSpec((1, d), lambda b, h, sl: (0, 0)),
                pl.BlockSpec((1, d), lambda b, h, sl: (0, 0)),
                pl.BlockSpec((1, d), lambda b, h, sl: (0, 0)),
                pl.BlockSpec((1, d), lambda b, h, sl: (0, 0)),
                pl.BlockSpec((1, e), lambda b, h, sl: (0, 0)),
            ],
            out_specs=pl.BlockSpec((S, e), lambda b, h, sl: (b, h)),
            scratch_shapes=[
                pltpu.VMEM((tq, tq), _f32),
                pltpu.VMEM((tq, tq), _f32),
                pltpu.VMEM((2, tq, e), _f32),
                pltpu.VMEM((2, tq, 1), _f32),
                pltpu.VMEM((2, tq, 1), _f32),
            ],
        ),
        out_shape=jax.ShapeDtypeStruct((B * S, H * e), _bf16),
        compiler_params=_params(2),
        name="diff_attention",
    )(slopes, qkv, qkv, qkv, vec(lq1), vec(lk1), vec(lq2), vec(lk2), vec(subln_g))


def _wo_kernel(o_ref, w_ref, res_ref, g_ref, h_ref, xn_ref):
    h = res_ref[...] + jnp.dot(o_ref[...], w_ref[...], preferred_element_type=_f32)
    h_ref[...] = h
    xn_ref[...] = _rms(h, g_ref[...]).astype(_bf16)


def _wo_proj(o, w, res, g, *, tm=512):
    T, D = res.shape
    return pl.pallas_call(
        _wo_kernel,
        grid=(T // tm,),
        in_specs=[
            pl.BlockSpec((tm, D), lambda i: (i, 0)),
            pl.BlockSpec((D, D), lambda i: (0, 0)),
            pl.BlockSpec((tm, D), lambda i: (i, 0)),
            pl.BlockSpec((1, D), lambda i: (0, 0)),
        ],
        out_specs=[
            pl.BlockSpec((tm, D), lambda i: (i, 0)),
            pl.BlockSpec((tm, D), lambda i: (i, 0)),
        ],
        out_shape=[
            jax.ShapeDtypeStruct((T, D), _f32),
            jax.ShapeDtypeStruct((T, D), _bf16),
        ],
        compiler_params=_params(1),
        name="wo_proj",
    )(o, w, res, g.reshape(1, D))


def _norm_kernel(h_ref, g_ref, o_ref):
    o_ref[...] = _rms(h_ref[...], g_ref[...])


def _final_norm(h, g, *, tm=1024):
    T, D = h.shape
    return pl.pallas_call(
        _norm_kernel,
        grid=(T // tm,),
        in_specs=[pl.BlockSpec((tm, D), lambda i: (i, 0)), pl.BlockSpec((1, D), lambda i: (0, 0))],
        out_specs=pl.BlockSpec((tm, D), lambda i: (i, 0)),
        out_shape=jax.ShapeDtypeStruct((T, D), _f32),
        compiler_params=_params(1),
        name="final_norm",
    )(h, g.reshape(1, D))


def kernel(x, norm_mix, norm_ffn, pool_w, pool_scale, w_qkv, lambda_q1, lambda_k1, lambda_q2, lambda_k2,
           subln_g, w_o, w_gate, w_up, w_down, final_norm):
    B, S, D = x.shape
    T = B * S
    H = D // V_HEAD_DIM
    bf = lambda a: a.astype(_bf16)

    h, xn = _pool_layer(x, norm_mix[0], bf(pool_w[0]), pool_scale[0], norm_ffn[0])
    h = h.reshape(T, D)
    hmid = _ffn_up(xn.reshape(T, D), bf(w_gate[0]), bf(w_up[0]))
    h = _ffn_down(hmid, bf(w_down[0]), h)

    lambda_init = 0.8 - 0.6 * math.exp(-0.3 * 1)
    colscale = jnp.concatenate([jnp.full((1, D), DIFF_HEAD_DIM ** -0.5, _f32), jnp.ones((1, 2 * D), _f32)], axis=1)
    qkv = _qkv_proj(h, norm_mix[1], bf(w_qkv[0]), colscale)
    slopes = jnp.asarray([2.0 ** (-8.0 * (i + 1) / H) for i in range(H)], dtype=_f32)
    o = _diff_attention(qkv, slopes, lambda_q1[0], lambda_k1[0], lambda_q2[0], lambda_k2[0], subln_g[0],
                        B=B, S=S, H=H, lambda_init=lambda_init)
    h, xn = _wo_proj(o, bf(w_o[0]), h, norm_ffn[1])
    hmid = _ffn_up(xn, bf(w_gate[1]), bf(w_up[1]))
    h = _ffn_down(hmid, bf(w_down[1]), h)
    return _final_norm(h, final_norm).reshape(B, S, D)
```

```python
import functools
import math
import struct

import jax
import jax.numpy as jnp
from jax import lax
from jax.experimental import pallas as pl
from jax.experimental.pallas import tpu as pltpu

RMS_EPS = 1e-6
POOL_WINDOWS = (2, 4, 8, 16)
POOL_HALO = 16
DIFF_HEAD_DIM = 128
V_HEAD_DIM = 2 * DIFF_HEAD_DIM
VMEM_LIMIT_BYTES = 56 * 1024 * 1024

_f32 = jnp.float32
_bf16 = jnp.bfloat16


def _params(n_axes):
    return pltpu.CompilerParams(
        dimension_semantics=("arbitrary",) * n_axes,
        vmem_limit_bytes=VMEM_LIMIT_BYTES,
    )


def _rms(x, g):
    ms = jnp.mean(x * x, axis=-1, keepdims=True)
    return x * lax.rsqrt(ms + RMS_EPS) * g


def _pool_kernel(x_ref, halo_ref, gmix_ref, w_ref, scale_ref, gffn_ref, h_ref, xn_ref, *, ts, group):
    s = pl.program_id(1)
    x = x_ref[0]
    g = gmix_ref[...]
    hn = _rms(x, g)
    halo = jnp.where(s > 0, _rms(halo_ref[0], g), 0.0)
    ext = jnp.concatenate([halo, hn], axis=0)
    t = s * ts + lax.broadcasted_iota(jnp.int32, (ts, 1), 0)
    ssq = jnp.zeros((ts, 1), _f32)
    for gi, w in enumerate(POOL_WINDOWS):
        cols = slice(gi * group, (gi + 1) * group)
        acc = ext[:, cols]
        span = 1
        while span < w:
            acc = acc + pltpu.roll(acc, span, axis=0)
            span *= 2
        cnt = jnp.minimum(t + 1, w).astype(_f32)
        pooled = acc[POOL_HALO:, :] / cnt
        diff = (pooled - hn[:, cols]).astype(_bf16)
        y = jnp.dot(diff, w_ref[gi], preferred_element_type=_f32)
        hg = x[:, cols] + y * scale_ref[:, cols]
        h_ref[0, :, cols] = hg
        ssq = ssq + jnp.sum(hg * hg, axis=-1, keepdims=True)
    d_model = x.shape[-1]
    inv = lax.rsqrt(ssq / d_model + RMS_EPS)
    xn_ref[0] = (h_ref[0] * inv * gffn_ref[...]).astype(_bf16)


def _pool_layer(x, g_mix, pool_w, pool_scale, g_ffn, *, ts=512):
    B, S, D = x.shape
    G, C, _ = pool_w.shape
    halo_blocks = ts // POOL_HALO
    kern = functools.partial(_pool_kernel, ts=ts, group=C)
    return pl.pallas_call(
        kern,
        grid=(B, S // ts),
        in_specs=[
            pl.BlockSpec((1, ts, D), lambda b, s: (b, s, 0)),
            pl.BlockSpec((1, POOL_HALO, D), lambda b, s: (b, jnp.maximum(s * halo_blocks - 1, 0), 0)),
            pl.BlockSpec((1, D), lambda b, s: (0, 0)),
            pl.BlockSpec((G, C, C), lambda b, s: (0, 0, 0)),
            pl.BlockSpec((1, D), lambda b, s: (0, 0)),
            pl.BlockSpec((1, D), lambda b, s: (0, 0)),
        ],
        out_specs=[
            pl.BlockSpec((1, ts, D), lambda b, s: (b, s, 0)),
            pl.BlockSpec((1, ts, D), lambda b, s: (b, s, 0)),
        ],
        out_shape=[
            jax.ShapeDtypeStruct((B, S, D), _f32),
            jax.ShapeDtypeStruct((B, S, D), _bf16),
        ],
        compiler_params=_params(2),
        name="pool_mixer",
    )(x, x, g_mix.reshape(1, D), pool_w, pool_scale.reshape(1, D), g_ffn.reshape(1, D))


def _ffn_up_kernel(xn_ref, wg_ref, wu_ref, o_ref):
    xn = xn_ref[...]
    gate = jnp.dot(xn, wg_ref[...], preferred_element_type=_f32)
    up = jnp.dot(xn, wu_ref[...], preferred_element_type=_f32)
    o_ref[...] = (gate / (1.0 + jnp.exp(-gate)) * up).astype(o_ref.dtype)


def _ffn_up(xn, wg, wu, *, tm=2048, tn=512):
    T, D = xn.shape
    F = wg.shape[1]
    return pl.pallas_call(
        _ffn_up_kernel,
        grid=(T // tm, F // tn),
        in_specs=[
            pl.BlockSpec((tm, D), lambda i, j: (i, 0)),
            pl.BlockSpec((D, tn), lambda i, j: (0, j)),
            pl.BlockSpec((D, tn), lambda i, j: (0, j)),
        ],
        out_specs=pl.BlockSpec((tm, tn), lambda i, j: (i, j)),
        out_shape=jax.ShapeDtypeStruct((T, F), _bf16),
        compiler_params=_params(2),
        name="ffn_up",
    )(xn, wg, wu)


def _ffn_down_kernel(hm_ref, wd_ref, res_ref, o_ref):
    o_ref[...] = res_ref[...] + jnp.dot(hm_ref[...], wd_ref[...], preferred_element_type=_f32)


def _ffn_down(hmid, wd, res, *, tm=1024, tn=512):
    T, F = hmid.shape
    D = wd.shape[1]
    return pl.pallas_call(
        _ffn_down_kernel,
        grid=(T // tm, D // tn),
        in_specs=[
            pl.BlockSpec((tm, F), lambda i, j: (i, 0)),
            pl.BlockSpec((F, tn), lambda i, j: (0, j)),
            pl.BlockSpec((tm, tn), lambda i, j: (i, j)),
        ],
        out_specs=pl.BlockSpec((tm, tn), lambda i, j: (i, j)),
        out_shape=jax.ShapeDtypeStruct((T, D), _f32),
        compiler_params=_params(2),
        name="ffn_down",
    )(hmid, wd, res)


def _qkv_kernel(h_ref, g_ref, w_ref, cs_ref, o_ref, xn_ref):
    @pl.when(pl.program_id(1) == 0)
    def _():
        xn_ref[...] = _rms(h_ref[...], g_ref[...]).astype(_bf16)

    acc = jnp.dot(xn_ref[...], w_ref[...], preferred_element_type=_f32)
    o_ref[...] = (acc * cs_ref[...]).astype(o_ref.dtype)


def _qkv_proj(h, g, w, colscale, *, tm=1024, tn=1536):
    T, D = h.shape
    N = w.shape[1]
    return pl.pallas_call(
        _qkv_kernel,
        grid=(T // tm, N // tn),
        in_specs=[
            pl.BlockSpec((tm, D), lambda i, j: (i, 0)),
            pl.BlockSpec((1, D), lambda i, j: (0, 0)),
            pl.BlockSpec((D, tn), lambda i, j: (0, j)),
            pl.BlockSpec((1, tn), lambda i, j: (0, j)),
        ],
        out_specs=pl.BlockSpec((tm, tn), lambda i, j: (i, j)),
        out_shape=jax.ShapeDtypeStruct((T, N), _bf16),
        scratch_shapes=[pltpu.VMEM((tm, D), _bf16)],
        compiler_params=_params(2),
        name="qkv_proj",
    )(h, g.reshape(1, D), w, colscale)


def _bf16_round(x):
    bits = struct.unpack("<I", struct.pack("<f", x))[0]
    bits = (bits + 0x7FFF + ((bits >> 16) & 1)) & 0xFFFF0000
    return struct.unpack("<f", struct.pack("<I", bits))[0]


def _bf16_split(x, n):
    terms = []
    for _ in range(n):
        t = _bf16_round(x)
        terms.append(t)
        x -= t
    return terms


LOG2E = math.log2(math.e)
LOG2E_TERMS = _bf16_split(LOG2E, 4)
KPOS_RADIX_BITS = 6


def _attn_kernel(slopes_ref, q_ref, k_ref, v_ref, lq1_ref, lk1_ref, lq2_ref, lk2_ref, sg_ref, o_ref,
                 kaug_ref, s_ref, p_ref, mrun_ref, m_ref, lrun_ref, l_ref, acc_ref, *, tq, lambda_init, unroll):
    d = DIFF_HEAD_DIM
    S = q_ref.shape[0]
    nq = S // tq
    nt = len(LOG2E_TERMS)
    slope = slopes_ref[pl.program_id(1)]

    pos = lax.broadcasted_iota(jnp.int32, (S, d), 0)
    lane = lax.broadcasted_iota(jnp.int32, (S, d), 1)
    lo = (pos & ((1 << KPOS_RADIX_BITS) - 1)).astype(_f32)
    hi = (pos >> KPOS_RADIX_BITS).astype(_f32) * float(1 << KPOS_RADIX_BITS)
    kaug_ref[...] = (slope * jnp.where(lane < nt, lo, jnp.where(lane < 2 * nt, hi, 0.0))).astype(_bf16)
    qlane = lax.broadcasted_iota(jnp.int32, (tq, d), 1)
    qaug = jnp.zeros((tq, d), _f32)
    for i, term in enumerate(LOG2E_TERMS):
        qaug = jnp.where((qlane == i) | (qlane == nt + i), term, qaug)
    qaug = qaug.astype(_bf16)

    lam = (jnp.exp(jnp.sum(lq1_ref[...] * lk1_ref[...], keepdims=True))
           - jnp.exp(jnp.sum(lq2_ref[...] * lk2_ref[...], keepdims=True))
           + lambda_init)

    def rows(j):
        if isinstance(j, int):
            return pl.ds(j * tq, tq)
        return pl.ds(pl.multiple_of(j * tq, tq), tq)

    def lane_blocks(x):
        return [x[:, c0:c0 + 128] for c0 in range(0, x.shape[1], 128)]

    def scores_chunk(u, j):
        qi, comp = divmod(u, 2)
        cs = slice(comp * d, (comp + 1) * d)
        qa = jnp.concatenate([q_ref[rows(qi), cs], qaug], axis=1)
        ka = jnp.concatenate([k_ref[rows(j), cs], kaug_ref[rows(j), :]], axis=1)
        s = lax.dot_general(qa, ka, (((1,), (1,)), ((), ())), preferred_element_type=_f32)
        if isinstance(j, int) and j == qi:
            r = lax.broadcasted_iota(jnp.int32, (tq, tq), 0)
            c = lax.broadcasted_iota(jnp.int32, (tq, tq), 1)
            s = jnp.where(c <= r, s, -jnp.inf)
        s_ref[comp, j] = s
        mrun_ref[comp] = functools.reduce(jnp.maximum, lane_blocks(s), mrun_ref[comp])

    def probs_chunk(u, j):
        comp = u % 2
        m = m_ref[comp]
        s = s_ref[comp, j]
        lsum = lrun_ref[comp]
        for b, sb in enumerate(lane_blocks(s)):
            pb = jnp.exp2(sb - m)
            lsum = lsum + pb
            p_ref[comp, j, :, b * 128:(b + 1) * 128] = pb.astype(_bf16)
        lrun_ref[comp] = lsum

    def pv_chunk(u, j):
        comp = u % 2
        acc_ref[comp] += jnp.dot(p_ref[comp, j], v_ref[rows(j), :], preferred_element_type=_f32)

    def start_scores(u):
        mrun_ref[u % 2] = jnp.full((tq, 128), -jnp.inf, _f32)

    def finish_scores(u):
        comp = u % 2
        m_ref[comp] = jnp.broadcast_to(jnp.max(mrun_ref[comp], axis=-1, keepdims=True), (tq, 128))
        lrun_ref[comp] = jnp.zeros((tq, 128), _f32)

    def finish_probs(u):
        comp = u % 2
        l_ref[comp] = jnp.broadcast_to(jnp.sum(lrun_ref[comp], axis=-1, keepdims=True), (tq, 128))
        acc_ref[comp] = jnp.zeros((tq, 2 * d), _f32)

    def finish_tile(qi):
        o = acc_ref[0] / l_ref[0][:, 0:1] - lam * (acc_ref[1] / l_ref[1][:, 0:1])
        ms = jnp.mean(o * o, axis=-1, keepdims=True)
        on = o * lax.rsqrt(ms + RMS_EPS) * sg_ref[...] * (1.0 - lambda_init)
        o_ref[rows(qi), :] = on.astype(o_ref.dtype)

    n_units = 2 * nq
    chunks = lambda u: u // 2 + 1 if 0 <= u < n_units else 0

    start_scores(0)
    for j in range(chunks(0)):
        scores_chunk(0, j)
    finish_scores(0)
    for u in range(n_units + 1):
        n_s, n_p, n_v = chunks(u + 1), chunks(u), chunks(u - 1)
        if n_s:
            start_scores(u + 1)
        n_common = min(n for n in (n_s, n_p, n_v) if n)
        if n_s:
            n_common = min(n_common, n_s - 1)

        def body(j, carry, u=u, n_s=n_s, n_p=n_p, n_v=n_v):
            if n_s:
                scores_chunk(u + 1, j)
            if n_p:
                probs_chunk(u, j)
            if n_v:
                pv_chunk(u - 1, j)
            return carry

        if n_common:
            lax.fori_loop(0, n_common, body, 0, unroll=min(unroll, n_common))
        for j in range(n_common, max(n_s, n_p, n_v)):
            if j < n_s:
                scores_chunk(u + 1, j)
            if j < n_p:
                probs_chunk(u, j)
            if j < n_v:
                pv_chunk(u - 1, j)
        if n_v and (u - 1) % 2 == 1:
            finish_tile((u - 1) // 2)
        if n_s:
            finish_scores(u + 1)
        if n_p:
            finish_probs(u)


def _diff_attention(qkv, slopes, lq1, lk1, lq2, lk2, subln_g, *, B, S, H, lambda_init, tq=512, unroll=2):
    d, e = DIFF_HEAD_DIM, V_HEAD_DIM
    kern = functools.partial(_attn_kernel, tq=tq, lambda_init=lambda_init, unroll=unroll)
    vec = lambda a: a.reshape(1, -1)
    return pl.pallas_call(
        kern,
        grid_spec=pltpu.PrefetchScalarGridSpec(
            num_scalar_prefetch=1,
            grid=(B, H),
            in_specs=[
                pl.BlockSpec((S, e), lambda b, h, sl: (b, h)),
                pl.BlockSpec((S, e), lambda b, h, sl: (b, H + h)),
                pl.BlockSpec((S, e), lambda b, h, sl: (b, 2 * H + h)),
                pl.BlockSpec((1, d), lambda b, h, sl: (0, 0)),
                pl.BlockSpec((1, d), lambda b, h, sl: (0, 0)),
                pl.BlockSpec((1, d), lambda b, h, sl: (0, 0)),
                pl.BlockSpec((1, d), lambda b, h, sl: (0, 0)),
                pl.BlockSpec((1, e), lambda b, h, sl: (0, 0)),
            ],
            out_specs=pl.BlockSpec((S, e), lambda b, h, sl: (b, h)),
            scratch_shapes=[
                pltpu.VMEM((S, d), _bf16),
                pltpu.VMEM((2, S // tq, tq, tq), _f32),
                pltpu.VMEM((2, S // tq, tq, tq), _bf16),
                pltpu.VMEM((2, tq, 128), _f32),
                pltpu.VMEM((2, tq, 128), _f32),
                pltpu.VMEM((2, tq, 128), _f32),
                pltpu.VMEM((2, tq, 128), _f32),
                pltpu.VMEM((2, tq, e), _f32),
            ],
        ),
        out_shape=jax.ShapeDtypeStruct((B * S, H * e), _bf16),
        compiler_params=_params(2),
        name="diff_attention",
    )(slopes, qkv, qkv, qkv, vec(lq1), vec(lk1), vec(lq2), vec(lk2), vec(subln_g))


def _wo_kernel(o_ref, w_ref, res_ref, g_ref, h_ref, xn_ref):
    h = res_ref[...] + jnp.dot(o_ref[...], w_ref[...], preferred_element_type=_f32)
    h_ref[...] = h
    xn_ref[...] = _rms(h, g_ref[...]).astype(_bf16)


def _wo_proj(o, w, res, g, *, tm=512):
    T, D = res.shape
    return pl.pallas_call(
        _wo_kernel,
        grid=(T // tm,),
        in_specs=[
            pl.BlockSpec((tm, D), lambda i: (i, 0)),
            pl.BlockSpec((D, D), lambda i: (0, 0)),
            pl.BlockSpec((tm, D), lambda i: (i, 0)),
            pl.BlockSpec((1, D), lambda i: (0, 0)),
        ],
        out_specs=[
            pl.BlockSpec((tm, D), lambda i: (i, 0)),
            pl.BlockSpec((tm, D), lambda i: (i, 0)),
        ],
        out_shape=[
            jax.ShapeDtypeStruct((T, D), _f32),
            jax.ShapeDtypeStruct((T, D), _bf16),
        ],
        compiler_params=_params(1),
        name="wo_proj",
    )(o, w, res, g.reshape(1, D))


def _norm_kernel(h_ref, g_ref, o_ref):
    o_ref[...] = _rms(h_ref[...], g_ref[...])


def _final_norm(h, g, *, tm=1024):
    T, D = h.shape
    return pl.pallas_call(
        _norm_kernel,
        grid=(T // tm,),
        in_specs=[pl.BlockSpec((tm, D), lambda i: (i, 0)), pl.BlockSpec((1, D), lambda i: (0, 0))],
        out_specs=pl.BlockSpec((tm, D), lambda i: (i, 0)),
        out_shape=jax.ShapeDtypeStruct((T, D), _f32),
        compiler_params=_params(1),
        name="final_norm",
    )(h, g.reshape(1, D))


def kernel(x, norm_mix, norm_ffn, pool_w, pool_scale, w_qkv, lambda_q1, lambda_k1, lambda_q2, lambda_k2,
           subln_g, w_o, w_gate, w_up, w_down, final_norm):
    B, S, D = x.shape
    T = B * S
    H = D // V_HEAD_DIM
    bf = lambda a: a.astype(_bf16)

    h, xn = _pool_layer(x, norm_mix[0], bf(pool_w[0]), pool_scale[0], norm_ffn[0])
    h = h.reshape(T, D)
    hmid = _ffn_up(xn.reshape(T, D), bf(w_gate[0]), bf(w_up[0]))
    h = _ffn_down(hmid, bf(w_down[0]), h)

    lambda_init = 0.8 - 0.6 * math.exp(-0.3 * 1)
    colscale = jnp.concatenate(
        [jnp.full((1, D), DIFF_HEAD_DIM ** -0.5 * LOG2E, _f32), jnp.ones((1, 2 * D), _f32)], axis=1)
    qkv = _qkv_proj(h, norm_mix[1], bf(w_qkv[0]), colscale)
    slopes = jnp.asarray([2.0 ** (-8.0 * (i + 1) / H) for i in range(H)], dtype=_f32)
    o = _diff_attention(qkv, slopes, lambda_q1[0], lambda_k1[0], lambda_q2[0], lambda_k2[0], subln_g[0],
                        B=B, S=S, H=H, lambda_init=lambda_init)
    h, xn = _wo_proj(o, bf(w_o[0]), h, norm_ffn[1])
    hmid = _ffn_up(xn, bf(w_gate[1]), bf(w_up[1]))
    h = _ffn_down(hmid, bf(w_down[1]), h)
    return _final_norm(h, final_norm).reshape(B, S, D)
```

```python
import functools
import math
import struct

import jax
import jax.numpy as jnp
from jax import lax
from jax.experimental import pallas as pl
from jax.experimental.pallas import tpu as pltpu

RMS_EPS = 1e-6
POOL_WINDOWS = (2, 4, 8, 16)
POOL_HALO = 16
DIFF_HEAD_DIM = 128
V_HEAD_DIM = 2 * DIFF_HEAD_DIM
LANES = 128
VMEM_LIMIT_BYTES = 56 * 1024 * 1024

_f32 = jnp.float32
_bf16 = jnp.bfloat16


def _params(n_axes):
    return pltpu.CompilerParams(
        dimension_semantics=("arbitrary",) * n_axes,
        vmem_limit_bytes=VMEM_LIMIT_BYTES,
    )


def _rows(a):
    return a.reshape(a.shape[0], 1, a.shape[1])


def _rms(x, g):
    ms = jnp.mean(x * x, axis=-1, keepdims=True)
    return x * lax.rsqrt(ms + RMS_EPS) * g


def _pool_kernel(x_ref, halo_ref, gmix_ref, w_ref, scale_ref, gffn_ref, h_ref, xn_ref, *, ts, group):
    s = pl.program_id(1)
    x = x_ref[0]
    g = gmix_ref[...]
    hn = _rms(x, g)
    halo = jnp.where(s > 0, _rms(halo_ref[0], g), 0.0)
    ext = jnp.concatenate([halo, hn], axis=0)
    t = s * ts + lax.broadcasted_iota(jnp.int32, (ts, 1), 0)
    ssq = jnp.zeros((ts, 1), _f32)
    for gi, w in enumerate(POOL_WINDOWS):
        cols = slice(gi * group, (gi + 1) * group)
        acc = ext[:, cols]
        span = 1
        while span < w:
            acc = acc + pltpu.roll(acc, span, axis=0)
            span *= 2
        cnt = jnp.minimum(t + 1, w).astype(_f32)
        pooled = acc[POOL_HALO:, :] / cnt
        diff = (pooled - hn[:, cols]).astype(_bf16)
        y = jnp.dot(diff, w_ref[gi].astype(_bf16), preferred_element_type=_f32)
        hg = x[:, cols] + y * scale_ref[:, cols]
        h_ref[0, :, cols] = hg
        ssq = ssq + jnp.sum(hg * hg, axis=-1, keepdims=True)
    d_model = x.shape[-1]
    inv = lax.rsqrt(ssq / d_model + RMS_EPS)
    xn_ref[0] = (h_ref[0] * inv * gffn_ref[...]).astype(_bf16)


def _pool_layer(x, norm_mix, pool_w, pool_scale, norm_ffn, *, layer, pool_layer, ts=512):
    B, S, D = x.shape
    _, G, C, _ = pool_w.shape
    halo_blocks = ts // POOL_HALO
    kern = functools.partial(_pool_kernel, ts=ts, group=C)
    return pl.pallas_call(
        kern,
        grid=(B, S // ts),
        in_specs=[
            pl.BlockSpec((1, ts, D), lambda b, s: (b, s, 0)),
            pl.BlockSpec((1, POOL_HALO, D), lambda b, s: (b, jnp.maximum(s * halo_blocks - 1, 0), 0)),
            pl.BlockSpec((None, 1, D), lambda b, s: (layer, 0, 0)),
            pl.BlockSpec((None, G, C, C), lambda b, s: (pool_layer, 0, 0, 0)),
            pl.BlockSpec((None, 1, D), lambda b, s: (pool_layer, 0, 0)),
            pl.BlockSpec((None, 1, D), lambda b, s: (layer, 0, 0)),
        ],
        out_specs=[
            pl.BlockSpec((1, ts, D), lambda b, s: (b, s, 0)),
            pl.BlockSpec((1, ts, D), lambda b, s: (b, s, 0)),
        ],
        out_shape=[
            jax.ShapeDtypeStruct((B, S, D), _f32),
            jax.ShapeDtypeStruct((B, S, D), _bf16),
        ],
        compiler_params=_params(2),
        name="pool_mixer",
    )(x, x, _rows(norm_mix), pool_w, _rows(pool_scale), _rows(norm_ffn))


def _ffn_up_kernel(xn_ref, wg_ref, wu_ref, wd_ref, o_ref, wd16_ref):
    xn = xn_ref[...]
    gate = jnp.dot(xn, wg_ref[...].astype(_bf16), preferred_element_type=_f32)
    up = jnp.dot(xn, wu_ref[...].astype(_bf16), preferred_element_type=_f32)
    o_ref[...] = (gate / (1.0 + jnp.exp(-gate)) * up).astype(o_ref.dtype)
    wd16_ref[...] = wd_ref[...].astype(_bf16)


def _ffn_up(xn, w_gate, w_up, w_down, *, layer, tm=2048, tn=512):
    T, D = xn.shape
    F = w_gate.shape[2]
    ni, nj = T // tm, F // tn
    slab = F // (ni * nj)
    assert slab * ni * nj == F and slab % 16 == 0
    return pl.pallas_call(
        _ffn_up_kernel,
        grid=(ni, nj),
        in_specs=[
            pl.BlockSpec((tm, D), lambda i, j: (i, 0)),
            pl.BlockSpec((None, D, tn), lambda i, j: (layer, 0, j)),
            pl.BlockSpec((None, D, tn), lambda i, j: (layer, 0, j)),
            pl.BlockSpec((None, slab, D), lambda i, j: (layer, i * nj + j, 0)),
        ],
        out_specs=[
            pl.BlockSpec((tm, tn), lambda i, j: (i, j)),
            pl.BlockSpec((slab, D), lambda i, j: (i * nj + j, 0)),
        ],
        out_shape=[
            jax.ShapeDtypeStruct((T, F), _bf16),
            jax.ShapeDtypeStruct((F, D), _bf16),
        ],
        compiler_params=_params(2),
        name="ffn_up",
    )(xn, w_gate, w_up, w_down)


def _ffn_down_kernel(hm_ref, wd_ref, res_ref, g_ref, *refs, emit_residual, n_side):
    side_in, outs = refs[:n_side], refs[n_side:]
    h = res_ref[...] + jnp.dot(hm_ref[...], wd_ref[...], preferred_element_type=_f32)
    if emit_residual:
        outs[0][...] = h
        outs[1][...] = _rms(h, g_ref[...]).astype(_bf16)
    else:
        outs[0][...] = _rms(h, g_ref[...])
    for src, dst in zip(side_in, outs[len(outs) - n_side:]):
        dst[...] = src[...].astype(_bf16)


def _ffn_down(hmid, wd16, res, g, g_row, *, emit_residual, side_w=None, tm=256):
    T, F = hmid.shape
    D = wd16.shape[1]
    n = T // tm
    in_specs = [
        pl.BlockSpec((tm, F), lambda i: (i, 0)),
        pl.BlockSpec((F, D), lambda i: (0, 0), pipeline_mode=pl.Buffered(1)),
        pl.BlockSpec((tm, D), lambda i: (i, 0)),
        pl.BlockSpec((None, 1, D), lambda i: (g_row, 0, 0)),
    ]
    row_spec = pl.BlockSpec((tm, D), lambda i: (i, 0))
    if emit_residual:
        out_specs = [row_spec, row_spec]
        out_shape = [jax.ShapeDtypeStruct((T, D), _f32), jax.ShapeDtypeStruct((T, D), _bf16)]
    else:
        out_specs = [row_spec]
        out_shape = [jax.ShapeDtypeStruct((T, D), _f32)]
    args = [hmid, wd16, res, _rows(g)]
    if side_w is not None:
        _, R, C = side_w.shape
        slab = R // n
        assert slab * n == R and slab % 16 == 0
        in_specs.append(pl.BlockSpec((None, slab, C), lambda i: (0, i, 0)))
        out_specs.append(pl.BlockSpec((slab, C), lambda i: (i, 0)))
        out_shape.append(jax.ShapeDtypeStruct((R, C), _bf16))
        args.append(side_w)
    kern = functools.partial(_ffn_down_kernel, emit_residual=emit_residual, n_side=int(side_w is not None))
    return pl.pallas_call(
        kern,
        grid=(n,),
        in_specs=in_specs,
        out_specs=out_specs,
        out_shape=out_shape,
        compiler_params=_params(1),
        name="ffn_down",
    )(*args)


def _qkv_kernel(xn_ref, w_ref, cs_ref, wo_ref, o_ref, wo16_ref):
    acc = jnp.dot(xn_ref[...], w_ref[...], preferred_element_type=_f32)
    o_ref[...] = (acc * cs_ref[...]).astype(o_ref.dtype)
    wo16_ref[...] = wo_ref[...].astype(_bf16)


def _qkv_proj(xn, w16, colscale, w_o, *, tm=1024, tn=1536):
    T, D = xn.shape
    N = w16.shape[1]
    ni, nj = T // tm, N // tn
    _, R, C = w_o.shape
    slab = R // (ni * nj)
    assert slab * ni * nj == R and slab % 16 == 0
    return pl.pallas_call(
        _qkv_kernel,
        grid=(ni, nj),
        in_specs=[
            pl.BlockSpec((tm, D), lambda i, j: (i, 0)),
            pl.BlockSpec((D, tn), lambda i, j: (0, j)),
            pl.BlockSpec((1, tn), lambda i, j: (0, j)),
            pl.BlockSpec((None, slab, C), lambda i, j: (0, i * nj + j, 0)),
        ],
        out_specs=[
            pl.BlockSpec((tm, tn), lambda i, j: (i, j)),
            pl.BlockSpec((slab, C), lambda i, j: (i * nj + j, 0)),
        ],
        out_shape=[
            jax.ShapeDtypeStruct((T, N), _bf16),
            jax.ShapeDtypeStruct((R, C), _bf16),
        ],
        compiler_params=_params(2),
        name="qkv_proj",
    )(xn, w16, colscale, w_o)


def _bf16_round(x):
    bits = struct.unpack("<I", struct.pack("<f", x))[0]
    bits = (bits + 0x7FFF + ((bits >> 16) & 1)) & 0xFFFF0000
    return struct.unpack("<f", struct.pack("<I", bits))[0]


def _bf16_split(x, n):
    terms = []
    for _ in range(n):
        t = _bf16_round(x)
        terms.append(t)
        x -= t
    return terms


LOG2E = math.log2(math.e)
LOG2E_TERMS = _bf16_split(LOG2E, 4)
KPOS_RADIX_BITS = 6


def _attn_kernel(slopes_ref, q_ref, k_ref, v_ref, lq1_ref, lk1_ref, lq2_ref, lk2_ref, sg_ref, o_ref,
                 kaug_ref, s_ref, p_ref, mrun_ref, m_ref, lrun_ref, l_ref, acc_ref, *, tq, lambda_init, unroll):
    d = DIFF_HEAD_DIM
    S = q_ref.shape[0]
    nq = S // tq
    nt = len(LOG2E_TERMS)
    slope = slopes_ref[pl.program_id(1)]

    pos = lax.broadcasted_iota(jnp.int32, (S, d), 0)
    lane = lax.broadcasted_iota(jnp.int32, (S, d), 1)
    lo = (pos & ((1 << KPOS_RADIX_BITS) - 1)).astype(_f32)
    hi = (pos >> KPOS_RADIX_BITS).astype(_f32) * float(1 << KPOS_RADIX_BITS)
    kaug_ref[...] = (slope * jnp.where(lane < nt, lo, jnp.where(lane < 2 * nt, hi, 0.0))).astype(_bf16)
    qlane = lax.broadcasted_iota(jnp.int32, (tq, d), 1)
    qaug = jnp.zeros((tq, d), _f32)
    for i, term in enumerate(LOG2E_TERMS):
        qaug = jnp.where((qlane == i) | (qlane == nt + i), term, qaug)
    qaug = qaug.astype(_bf16)

    lam = (jnp.exp(jnp.sum(lq1_ref[...] * lk1_ref[...], keepdims=True))
           - jnp.exp(jnp.sum(lq2_ref[...] * lk2_ref[...], keepdims=True))
           + lambda_init)

    def rows(j):
        if isinstance(j, int):
            return pl.ds(j * tq, tq)
        return pl.ds(pl.multiple_of(j * tq, tq), tq)

    def lane_blocks(x):
        return [x[:, c0:c0 + LANES] for c0 in range(0, x.shape[1], LANES)]

    def scores_chunk(u, j):
        qi, comp = divmod(u, 2)
        cs = slice(comp * d, (comp + 1) * d)
        qa = jnp.concatenate([q_ref[rows(qi), cs], qaug], axis=1)
        ka = jnp.concatenate([k_ref[rows(j), cs], kaug_ref[rows(j), :]], axis=1)
        s = lax.dot_general(qa, ka, (((1,), (1,)), ((), ())), preferred_element_type=_f32)
        if isinstance(j, int) and j == qi:
            r = lax.broadcasted_iota(jnp.int32, (tq, tq), 0)
            c = lax.broadcasted_iota(jnp.int32, (tq, tq), 1)
            s = jnp.where(c <= r, s, -jnp.inf)
        s_ref[comp, j] = s
        mrun_ref[comp] = functools.reduce(jnp.maximum, lane_blocks(s), mrun_ref[comp])

    def probs_chunk(u, j):
        comp = u % 2
        m = m_ref[comp]
        s = s_ref[comp, j]
        lsum = lrun_ref[comp]
        for b, sb in enumerate(lane_blocks(s)):
            pb = jnp.exp2(sb - m)
            lsum = lsum + pb
            p_ref[comp, j, :, b * LANES:(b + 1) * LANES] = pb.astype(_bf16)
        lrun_ref[comp] = lsum

    def pv_chunk(u, j):
        comp = u % 2
        acc_ref[comp] += jnp.dot(p_ref[comp, j], v_ref[rows(j), :], preferred_element_type=_f32)

    def start_scores(u):
        mrun_ref[u % 2] = jnp.full((tq, LANES), -jnp.inf, _f32)

    def finish_scores(u):
        comp = u % 2
        m_ref[comp] = jnp.broadcast_to(jnp.max(mrun_ref[comp], axis=-1, keepdims=True), (tq, LANES))
        lrun_ref[comp] = jnp.zeros((tq, LANES), _f32)

    def finish_probs(u):
        comp = u % 2
        l_ref[comp] = jnp.broadcast_to(jnp.sum(lrun_ref[comp], axis=-1, keepdims=True), (tq, LANES))
        acc_ref[comp] = jnp.zeros((tq, 2 * d), _f32)

    def finish_tile(qi):
        o = acc_ref[0] / l_ref[0][:, 0:1] - lam * (acc_ref[1] / l_ref[1][:, 0:1])
        ms = jnp.mean(o * o, axis=-1, keepdims=True)
        on = o * lax.rsqrt(ms + RMS_EPS) * sg_ref[...] * (1.0 - lambda_init)
        o_ref[rows(qi), :] = on.astype(o_ref.dtype)

    n_units = 2 * nq
    chunks = lambda u: u // 2 + 1 if 0 <= u < n_units else 0

    start_scores(0)
    for j in range(chunks(0)):
        scores_chunk(0, j)
    finish_scores(0)
    for u in range(n_units + 1):
        n_s, n_p, n_v = chunks(u + 1), chunks(u), chunks(u - 1)
        if n_s:
            start_scores(u + 1)
        n_common = min(n for n in (n_s, n_p, n_v) if n)
        if n_s:
            n_common = min(n_common, n_s - 1)

        def body(j, carry, u=u, n_s=n_s, n_p=n_p, n_v=n_v):
            if n_s:
                scores_chunk(u + 1, j)
            if n_p:
                probs_chunk(u, j)
            if n_v:
                pv_chunk(u - 1, j)
            return carry

        if n_common:
            lax.fori_loop(0, n_common, body, 0, unroll=min(unroll, n_common))
        for j in range(n_common, max(n_s, n_p, n_v)):
            if j < n_s:
                scores_chunk(u + 1, j)
            if j < n_p:
                probs_chunk(u, j)
            if j < n_v:
                pv_chunk(u - 1, j)
        if n_v and (u - 1) % 2 == 1:
            finish_tile((u - 1) // 2)
        if n_s:
            finish_scores(u + 1)
        if n_p:
            finish_probs(u)


def _diff_attention(qkv, slopes, lq1, lk1, lq2, lk2, subln_g, *, B, S, H, lambda_init, attn_layer, tq=512, unroll=2):
    d, e = DIFF_HEAD_DIM, V_HEAD_DIM
    kern = functools.partial(_attn_kernel, tq=tq, lambda_init=lambda_init, unroll=unroll)
    return pl.pallas_call(
        kern,
        grid_spec=pltpu.PrefetchScalarGridSpec(
            num_scalar_prefetch=1,
            grid=(B, H),
            in_specs=[
                pl.BlockSpec((S, e), lambda b, h, sl: (b, h)),
                pl.BlockSpec((S, e), lambda b, h, sl: (b, H + h)),
                pl.BlockSpec((S, e), lambda b, h, sl: (b, 2 * H + h)),
                pl.BlockSpec((None, 1, d), lambda b, h, sl: (attn_layer, 0, 0)),
                pl.BlockSpec((None, 1, d), lambda b, h, sl: (attn_layer, 0, 0)),
                pl.BlockSpec((None, 1, d), lambda b, h, sl: (attn_layer, 0, 0)),
                pl.BlockSpec((None, 1, d), lambda b, h, sl: (attn_layer, 0, 0)),
                pl.BlockSpec((None, 1, e), lambda b, h, sl: (attn_layer, 0, 0)),
            ],
            out_specs=pl.BlockSpec((S, e), lambda b, h, sl: (b, h)),
            scratch_shapes=[
                pltpu.VMEM((S, d), _bf16),
                pltpu.VMEM((2, S // tq, tq, tq), _f32),
                pltpu.VMEM((2, S // tq, tq, tq), _bf16),
                pltpu.VMEM((2, tq, LANES), _f32),
                pltpu.VMEM((2, tq, LANES), _f32),
                pltpu.VMEM((2, tq, LANES), _f32),
                pltpu.VMEM((2, tq, LANES), _f32),
                pltpu.VMEM((2, tq, e), _f32),
            ],
        ),
        out_shape=jax.ShapeDtypeStruct((B * S, H * e), _bf16),
        compiler_params=_params(2),
        name="diff_attention",
    )(slopes, qkv, qkv, qkv, _rows(lq1), _rows(lk1), _rows(lq2), _rows(lk2), _rows(subln_g))


def _wo_kernel(o_ref, w_ref, res_ref, g_ref, h_ref, xn_ref):
    h = res_ref[...] + jnp.dot(o_ref[...], w_ref[...], preferred_element_type=_f32)
    h_ref[...] = h
    xn_ref[...] = _rms(h, g_ref[...]).astype(_bf16)


def _wo_proj(o, w16, res, norm_ffn, *, layer, tm=512):
    T, D = res.shape
    return pl.pallas_call(
        _wo_kernel,
        grid=(T // tm,),
        in_specs=[
            pl.BlockSpec((tm, D), lambda i: (i, 0)),
            pl.BlockSpec((D, D), lambda i: (0, 0), pipeline_mode=pl.Buffered(1)),
            pl.BlockSpec((tm, D), lambda i: (i, 0)),
            pl.BlockSpec((None, 1, D), lambda i: (layer, 0, 0)),
        ],
        out_specs=[
            pl.BlockSpec((tm, D), lambda i: (i, 0)),
            pl.BlockSpec((tm, D), lambda i: (i, 0)),
        ],
        out_shape=[
            jax.ShapeDtypeStruct((T, D), _f32),
            jax.ShapeDtypeStruct((T, D), _bf16),
        ],
        compiler_params=_params(1),
        name="wo_proj",
    )(o, w16, res, _rows(norm_ffn))


def kernel(x, norm_mix, norm_ffn, pool_w, pool_scale, w_qkv, lambda_q1, lambda_k1, lambda_q2, lambda_k2,
           subln_g, w_o, w_gate, w_up, w_down, final_norm):
    B, S, D = x.shape
    T = B * S
    H = D // V_HEAD_DIM

    h, xn = _pool_layer(x, norm_mix, pool_w, pool_scale, norm_ffn, layer=0, pool_layer=0)
    hmid, wd16 = _ffn_up(xn.reshape(T, D), w_gate, w_up, w_down, layer=0)
    h, xn, wqkv16 = _ffn_down(hmid, wd16, h.reshape(T, D), norm_mix, 1, emit_residual=True, side_w=w_qkv)

    lambda_init = 0.8 - 0.6 * math.exp(-0.3 * 1)
    colscale = jnp.concatenate(
        [jnp.full((1, D), DIFF_HEAD_DIM ** -0.5 * LOG2E, _f32), jnp.ones((1, 2 * D), _f32)], axis=1)
    qkv, wo16 = _qkv_proj(xn, wqkv16, colscale, w_o)
    slopes = jnp.asarray([2.0 ** (-8.0 * (i + 1) / H) for i in range(H)], dtype=_f32)
    o = _diff_attention(qkv, slopes, lambda_q1, lambda_k1, lambda_q2, lambda_k2, subln_g,
                        B=B, S=S, H=H, lambda_init=lambda_init, attn_layer=0)
    h, xn = _wo_proj(o, wo16, h, norm_ffn, layer=1)
    hmid, wd16 = _ffn_up(xn, w_gate, w_up, w_down, layer=1)
    (out,) = _ffn_down(hmid, wd16, h, final_norm.reshape(1, D), 0, emit_residual=False)
    return out.reshape(B, S, D)
```

```python
import functools
import math
import struct

import jax
import jax.numpy as jnp
from jax import lax
from jax.experimental import pallas as pl
from jax.experimental.pallas import tpu as pltpu

RMS_EPS = 1e-6
POOL_WINDOWS = (2, 4, 8, 16)
POOL_HALO = 16
DIFF_HEAD_DIM = 128
V_HEAD_DIM = 2 * DIFF_HEAD_DIM
LANES = 128
ROW_GROUP = 64
VMEM_LIMIT_BYTES = 56 * 1024 * 1024

_f32 = jnp.float32
_bf16 = jnp.bfloat16


def _params(n_axes, flags=None):
    return pltpu.CompilerParams(
        dimension_semantics=("arbitrary",) * n_axes,
        vmem_limit_bytes=VMEM_LIMIT_BYTES,
        flags=flags,
    )


def _rows(a):
    return a.reshape(a.shape[0], 1, a.shape[1])


def _rms(x, g):
    ms = jnp.mean(x * x, axis=-1, keepdims=True)
    return x * lax.rsqrt(ms + RMS_EPS) * g


def _pool_kernel(x_ref, halo_ref, gmix_ref, w_ref, scale_ref, gffn_ref, h_ref, xn_ref, *, ts, group):
    s = pl.program_id(1)
    x = x_ref[0]
    g = gmix_ref[...]
    hn = _rms(x, g)
    halo = jnp.where(s > 0, _rms(halo_ref[0], g), 0.0)
    ext = jnp.concatenate([halo, hn], axis=0)
    t = s * ts + lax.broadcasted_iota(jnp.int32, (ts, 1), 0)
    ssq = jnp.zeros((ts, 1), _f32)
    for gi, w in enumerate(POOL_WINDOWS):
        cols = slice(gi * group, (gi + 1) * group)
        acc = ext[:, cols]
        span = 1
        while span < w:
            acc = acc + pltpu.roll(acc, span, axis=0)
            span *= 2
        cnt = jnp.minimum(t + 1, w).astype(_f32)
        pooled = acc[POOL_HALO:, :] / cnt
        diff = (pooled - hn[:, cols]).astype(_bf16)
        y = jnp.dot(diff, w_ref[gi].astype(_bf16), preferred_element_type=_f32)
        hg = x[:, cols] + y * scale_ref[:, cols]
        h_ref[0, :, cols] = hg
        ssq = ssq + jnp.sum(hg * hg, axis=-1, keepdims=True)
    d_model = x.shape[-1]
    inv = lax.rsqrt(ssq / d_model + RMS_EPS)
    xn_ref[0] = (h_ref[0] * inv * gffn_ref[...]).astype(_bf16)


def _pool_layer(x, norm_mix, pool_w, pool_scale, norm_ffn, *, layer, pool_layer, ts=512):
    B, S, D = x.shape
    _, G, C, _ = pool_w.shape
    halo_blocks = ts // POOL_HALO
    kern = functools.partial(_pool_kernel, ts=ts, group=C)
    return pl.pallas_call(
        kern,
        grid=(B, S // ts),
        in_specs=[
            pl.BlockSpec((1, ts, D), lambda b, s: (b, s, 0)),
            pl.BlockSpec((1, POOL_HALO, D), lambda b, s: (b, jnp.maximum(s * halo_blocks - 1, 0), 0)),
            pl.BlockSpec((None, 1, D), lambda b, s: (layer, 0, 0)),
            pl.BlockSpec((None, G, C, C), lambda b, s: (pool_layer, 0, 0, 0)),
            pl.BlockSpec((None, 1, D), lambda b, s: (pool_layer, 0, 0)),
            pl.BlockSpec((None, 1, D), lambda b, s: (layer, 0, 0)),
        ],
        out_specs=[
            pl.BlockSpec((1, ts, D), lambda b, s: (b, s, 0)),
            pl.BlockSpec((1, ts, D), lambda b, s: (b, s, 0)),
        ],
        out_shape=[
            jax.ShapeDtypeStruct((B, S, D), _f32),
            jax.ShapeDtypeStruct((B, S, D), _bf16),
        ],
        compiler_params=_params(2),
        name="pool_mixer",
    )(x, x, _rows(norm_mix), pool_w, _rows(pool_scale), _rows(norm_ffn))


def _ffn_up_kernel(xn_ref, wg_ref, wu_ref, wd_ref, o_ref, wd16_ref):
    xn = xn_ref[...]
    gate = jnp.dot(xn, wg_ref[...].astype(_bf16), preferred_element_type=_f32)
    up = jnp.dot(xn, wu_ref[...].astype(_bf16), preferred_element_type=_f32)
    o_ref[...] = (gate / (1.0 + jnp.exp(-gate)) * up).astype(o_ref.dtype)
    wd16_ref[...] = wd_ref[...].astype(_bf16)


def _ffn_up(xn, w_gate, w_up, w_down, *, layer, tm=2048, tn=512):
    T, D = xn.shape
    F = w_gate.shape[2]
    ni, nj = T // tm, F // tn
    slab = F // (ni * nj)
    assert slab * ni * nj == F and slab % 16 == 0
    return pl.pallas_call(
        _ffn_up_kernel,
        grid=(ni, nj),
        in_specs=[
            pl.BlockSpec((tm, D), lambda i, j: (i, 0)),
            pl.BlockSpec((None, D, tn), lambda i, j: (layer, 0, j)),
            pl.BlockSpec((None, D, tn), lambda i, j: (layer, 0, j)),
            pl.BlockSpec((None, slab, D), lambda i, j: (layer, i * nj + j, 0)),
        ],
        out_specs=[
            pl.BlockSpec((tm, tn), lambda i, j: (i, j)),
            pl.BlockSpec((slab, D), lambda i, j: (i * nj + j, 0)),
        ],
        out_shape=[
            jax.ShapeDtypeStruct((T, F), _bf16),
            jax.ShapeDtypeStruct((F, D), _bf16),
        ],
        compiler_params=_params(2),
        name="ffn_up",
    )(xn, w_gate, w_up, w_down)


def _ffn_down_kernel(hm_ref, wd_ref, res_ref, g_ref, *refs, emit_residual, n_side):
    side_in, outs = refs[:n_side], refs[n_side:]
    h = res_ref[...] + jnp.dot(hm_ref[...], wd_ref[...], preferred_element_type=_f32)
    if emit_residual:
        outs[0][...] = h
        outs[1][...] = _rms(h, g_ref[...]).astype(_bf16)
    else:
        outs[0][...] = _rms(h, g_ref[...])
    for src, dst in zip(side_in, outs[len(outs) - n_side:]):
        dst[...] = src[...].astype(_bf16)


def _ffn_down(hmid, wd16, res, g, g_row, *, emit_residual, side_w=None, tm=256):
    T, F = hmid.shape
    D = wd16.shape[1]
    n = T // tm
    in_specs = [
        pl.BlockSpec((tm, F), lambda i: (i, 0)),
        pl.BlockSpec((F, D), lambda i: (0, 0), pipeline_mode=pl.Buffered(1)),
        pl.BlockSpec((tm, D), lambda i: (i, 0)),
        pl.BlockSpec((None, 1, D), lambda i: (g_row, 0, 0)),
    ]
    row_spec = pl.BlockSpec((tm, D), lambda i: (i, 0))
    if emit_residual:
        out_specs = [row_spec, row_spec]
        out_shape = [jax.ShapeDtypeStruct((T, D), _f32), jax.ShapeDtypeStruct((T, D), _bf16)]
    else:
        out_specs = [row_spec]
        out_shape = [jax.ShapeDtypeStruct((T, D), _f32)]
    args = [hmid, wd16, res, _rows(g)]
    if side_w is not None:
        _, R, C = side_w.shape
        slab = R // n
        assert slab * n == R and slab % 16 == 0
        in_specs.append(pl.BlockSpec((None, slab, C), lambda i: (0, i, 0)))
        out_specs.append(pl.BlockSpec((slab, C), lambda i: (i, 0)))
        out_shape.append(jax.ShapeDtypeStruct((R, C), _bf16))
        args.append(side_w)
    kern = functools.partial(_ffn_down_kernel, emit_residual=emit_residual, n_side=int(side_w is not None))
    return pl.pallas_call(
        kern,
        grid=(n,),
        in_specs=in_specs,
        out_specs=out_specs,
        out_shape=out_shape,
        compiler_params=_params(1),
        name="ffn_down",
    )(*args)


def _qkv_kernel(xn_ref, w_ref, cs_ref, wo_ref, o_ref, wo16_ref):
    acc = jnp.dot(xn_ref[...], w_ref[...], preferred_element_type=_f32)
    o_ref[...] = (acc * cs_ref[...]).astype(o_ref.dtype)
    wo16_ref[...] = wo_ref[...].astype(_bf16)


def _qkv_proj(xn, w16, colscale, w_o, *, tm=1024, tn=1536):
    T, D = xn.shape
    N = w16.shape[1]
    ni, nj = T // tm, N // tn
    _, R, C = w_o.shape
    slab = R // (ni * nj)
    assert slab * ni * nj == R and slab % 16 == 0
    return pl.pallas_call(
        _qkv_kernel,
        grid=(ni, nj),
        in_specs=[
            pl.BlockSpec((tm, D), lambda i, j: (i, 0)),
            pl.BlockSpec((D, tn), lambda i, j: (0, j)),
            pl.BlockSpec((1, tn), lambda i, j: (0, j)),
            pl.BlockSpec((None, slab, C), lambda i, j: (0, i * nj + j, 0)),
        ],
        out_specs=[
            pl.BlockSpec((tm, tn), lambda i, j: (i, j)),
            pl.BlockSpec((slab, C), lambda i, j: (i * nj + j, 0)),
        ],
        out_shape=[
            jax.ShapeDtypeStruct((T, N), _bf16),
            jax.ShapeDtypeStruct((R, C), _bf16),
        ],
        compiler_params=_params(2),
        name="qkv_proj",
    )(xn, w16, colscale, w_o)


def _bf16_round(x):
    bits = struct.unpack("<I", struct.pack("<f", x))[0]
    bits = (bits + 0x7FFF + ((bits >> 16) & 1)) & 0xFFFF0000
    return struct.unpack("<f", struct.pack("<I", bits))[0]


def _bf16_split(x, n):
    terms = []
    for _ in range(n):
        t = _bf16_round(x)
        terms.append(t)
        x -= t
    return terms


LOG2E = math.log2(math.e)
LOG2E_TERMS = _bf16_split(LOG2E, 4)
KPOS_RADIX_BITS = 6


def _attn_kernel(slopes_ref, q_ref, k_ref, v_ref, lq1_ref, lk1_ref, lq2_ref, lk2_ref, sg_ref, o_ref,
                 kaug_ref, qaug_ref, s0_ref, s1_ref, p0_ref, p1_ref, mrun_ref, m_ref, lrun_ref, l_ref, acc_ref,
                 *, tq, lambda_init, unroll):
    d = DIFF_HEAD_DIM
    S = q_ref.shape[0]
    nq = S // tq
    nt = len(LOG2E_TERMS)
    slope = slopes_ref[pl.program_id(1)]
    s_refs, p_refs = (s0_ref, s1_ref), (p0_ref, p1_ref)

    pos = lax.broadcasted_iota(jnp.int32, (S, d), 0)
    lane = lax.broadcasted_iota(jnp.int32, (S, d), 1)
    lo = (pos & ((1 << KPOS_RADIX_BITS) - 1)).astype(_f32)
    hi = (pos >> KPOS_RADIX_BITS).astype(_f32) * float(1 << KPOS_RADIX_BITS)
    kaug_ref[...] = (slope * jnp.where(lane < nt, lo, jnp.where(lane < 2 * nt, hi, 0.0))).astype(_bf16)
    qlane = lax.broadcasted_iota(jnp.int32, (tq, d), 1)
    qaug = jnp.zeros((tq, d), _f32)
    for i, term in enumerate(LOG2E_TERMS):
        qaug = jnp.where((qlane == i) | (qlane == nt + i), term, qaug)
    qaug_ref[...] = qaug.astype(_bf16)

    lam = (jnp.exp(jnp.sum(lq1_ref[...] * lk1_ref[...], keepdims=True))
           - jnp.exp(jnp.sum(lq2_ref[...] * lk2_ref[...], keepdims=True))
           + lambda_init)

    def rows(j):
        if isinstance(j, int):
            return pl.ds(j * tq, tq)
        return pl.ds(pl.multiple_of(j * tq, tq), tq)

    def lane_blocks(x):
        return [x[:, c0:c0 + LANES] for c0 in range(0, x.shape[1], LANES)]

    def scores_chunk(u, j):
        qi, comp = divmod(u, 2)
        cs = slice(comp * d, (comp + 1) * d)
        qa = jnp.concatenate([q_ref[rows(qi), cs], qaug_ref[...]], axis=1)
        ka = jnp.concatenate([k_ref[rows(j), cs], kaug_ref[rows(j), :]], axis=1)
        s = lax.dot_general(qa, ka, (((1,), (1,)), ((), ())), preferred_element_type=_f32)
        if isinstance(j, int) and j == qi:
            r = lax.broadcasted_iota(jnp.int32, (tq, tq), 0)
            c = lax.broadcasted_iota(jnp.int32, (tq, tq), 1)
            s = jnp.where(c <= r, s, -jnp.inf)
        s_refs[comp][j] = s
        mrun_ref[comp] = functools.reduce(jnp.maximum, lane_blocks(s), mrun_ref[comp])

    def probs_chunk(u, j):
        comp = u % 2
        for r0 in range(0, tq, ROW_GROUP):
            rs = slice(r0, r0 + ROW_GROUP)
            m = m_ref[comp, rs, :]
            lsum = lrun_ref[comp, rs, :]
            for b, sb in enumerate(lane_blocks(s_refs[comp][j, rs, :])):
                pb = jnp.exp2(sb - m)
                lsum = lsum + pb
                p_refs[comp][j, rs, b * LANES:(b + 1) * LANES] = pb.astype(_bf16)
            lrun_ref[comp, rs, :] = lsum

    def pv_chunk(u, j):
        comp = u % 2
        acc_ref[comp] += jnp.dot(p_refs[comp][j], v_ref[rows(j), :], preferred_element_type=_f32)

    def start_scores(u):
        mrun_ref[u % 2] = jnp.full((tq, LANES), -jnp.inf, _f32)

    def finish_scores(u):
        comp = u % 2
        m_ref[comp] = jnp.broadcast_to(jnp.max(mrun_ref[comp], axis=-1, keepdims=True), (tq, LANES))
        lrun_ref[comp] = jnp.zeros((tq, LANES), _f32)

    def finish_probs(u):
        comp = u % 2
        l_ref[comp] = jnp.broadcast_to(jnp.sum(lrun_ref[comp], axis=-1, keepdims=True), (tq, LANES))
        acc_ref[comp] = jnp.zeros((tq, 2 * d), _f32)

    def finish_tile(qi):
        gain = sg_ref[...] * (1.0 - lambda_init)
        for r0 in range(0, tq, ROW_GROUP):
            rs = slice(r0, r0 + ROW_GROUP)
            inv = [1.0 / l_ref[comp, rs, :] for comp in range(2)]
            inv = [jnp.concatenate([x] * (2 * d // LANES), axis=1) for x in inv]
            o = acc_ref[0, rs, :] * inv[0] - lam * (acc_ref[1, rs, :] * inv[1])
            ms = jnp.mean(o * o, axis=-1, keepdims=True)
            on = o * lax.rsqrt(ms + RMS_EPS) * gain
            o_ref[pl.ds(qi * tq + r0, ROW_GROUP), :] = on.astype(o_ref.dtype)

    n_units = 2 * nq
    chunks = lambda u: u // 2 + 1 if 0 <= u < n_units else 0

    start_scores(0)
    for j in range(chunks(0)):
        scores_chunk(0, j)
    finish_scores(0)
    for u in range(n_units + 1):
        n_s, n_p, n_v = chunks(u + 1), chunks(u), chunks(u - 1)
        if n_s:
            start_scores(u + 1)
        n_common = min(n for n in (n_s, n_p, n_v) if n)
        if n_s:
            n_common = min(n_common, n_s - 1)

        def chunk_step(j, u=u, n_s=n_s, n_p=n_p, n_v=n_v):
            if n_s:
                scores_chunk(u + 1, j)
            if n_p:
                probs_chunk(u, j)
            if n_v:
                pv_chunk(u - 1, j)

        def body(j, carry, chunk_step=chunk_step):
            chunk_step(j)
            return carry

        if n_common:
            lax.fori_loop(0, n_common, body, 0, unroll=min(unroll, n_common))
        for j in range(n_common, max(n_s, n_p, n_v)):
            if j < n_s:
                scores_chunk(u + 1, j)
            if j < n_p:
                probs_chunk(u, j)
            if j < n_v:
                pv_chunk(u - 1, j)
        if n_v and (u - 1) % 2 == 1:
            finish_tile((u - 1) // 2)
        if n_s:
            finish_scores(u + 1)
        if n_p:
            finish_probs(u)


def _diff_attention(qkv, slopes, lq1, lk1, lq2, lk2, subln_g, *, B, S, H, lambda_init, attn_layer, tq=512, unroll=2):
    d, e = DIFF_HEAD_DIM, V_HEAD_DIM
    kern = functools.partial(_attn_kernel, tq=tq, lambda_init=lambda_init, unroll=unroll)
    return pl.pallas_call(
        kern,
        grid_spec=pltpu.PrefetchScalarGridSpec(
            num_scalar_prefetch=1,
            grid=(B, H),
            in_specs=[
                pl.BlockSpec((S, e), lambda b, h, sl: (b, h)),
                pl.BlockSpec((S, e), lambda b, h, sl: (b, H + h)),
                pl.BlockSpec((S, e), lambda b, h, sl: (b, 2 * H + h)),
                pl.BlockSpec((None, 1, d), lambda b, h, sl: (attn_layer, 0, 0)),
                pl.BlockSpec((None, 1, d), lambda b, h, sl: (attn_layer, 0, 0)),
                pl.BlockSpec((None, 1, d), lambda b, h, sl: (attn_layer, 0, 0)),
                pl.BlockSpec((None, 1, d), lambda b, h, sl: (attn_layer, 0, 0)),
                pl.BlockSpec((None, 1, e), lambda b, h, sl: (attn_layer, 0, 0)),
            ],
            out_specs=pl.BlockSpec((S, e), lambda b, h, sl: (b, h)),
            scratch_shapes=[
                pltpu.VMEM((S, d), _bf16),
                pltpu.VMEM((tq, d), _bf16),
                pltpu.VMEM((S // tq, tq, tq), _f32),
                pltpu.VMEM((S // tq, tq, tq), _f32),
                pltpu.VMEM((S // tq, tq, tq), _bf16),
                pltpu.VMEM((S // tq, tq, tq), _bf16),
                pltpu.VMEM((2, tq, LANES), _f32),
                pltpu.VMEM((2, tq, LANES), _f32),
                pltpu.VMEM((2, tq, LANES), _f32),
                pltpu.VMEM((2, tq, LANES), _f32),
                pltpu.VMEM((2, tq, e), _f32),
            ],
        ),
        out_shape=jax.ShapeDtypeStruct((B * S, H * e), _bf16),
        compiler_params=_params(2),
        name="diff_attention",
    )(slopes, qkv, qkv, qkv, _rows(lq1), _rows(lk1), _rows(lq2), _rows(lk2), _rows(subln_g))


def _wo_kernel(o_ref, w_ref, res_ref, g_ref, h_ref, xn_ref):
    h = res_ref[...] + jnp.dot(o_ref[...], w_ref[...], preferred_element_type=_f32)
    h_ref[...] = h
    xn_ref[...] = _rms(h, g_ref[...]).astype(_bf16)


def _wo_proj(o, w16, res, norm_ffn, *, layer, tm=512):
    T, D = res.shape
    return pl.pallas_call(
        _wo_kernel,
        grid=(T // tm,),
        in_specs=[
            pl.BlockSpec((tm, D), lambda i: (i, 0)),
            pl.BlockSpec((D, D), lambda i: (0, 0), pipeline_mode=pl.Buffered(1)),
            pl.BlockSpec((tm, D), lambda i: (i, 0)),
            pl.BlockSpec((None, 1, D), lambda i: (layer, 0, 0)),
        ],
        out_specs=[
            pl.BlockSpec((tm, D), lambda i: (i, 0)),
            pl.BlockSpec((tm, D), lambda i: (i, 0)),
        ],
        out_shape=[
            jax.ShapeDtypeStruct((T, D), _f32),
            jax.ShapeDtypeStruct((T, D), _bf16),
        ],
        compiler_params=_params(1),
        name="wo_proj",
    )(o, w16, res, _rows(norm_ffn))


def kernel(x, norm_mix, norm_ffn, pool_w, pool_scale, w_qkv, lambda_q1, lambda_k1, lambda_q2, lambda_k2,
           subln_g, w_o, w_gate, w_up, w_down, final_norm):
    B, S, D = x.shape
    T = B * S
    H = D // V_HEAD_DIM

    h, xn = _pool_layer(x, norm_mix, pool_w, pool_scale, norm_ffn, layer=0, pool_layer=0)
    hmid, wd16 = _ffn_up(xn.reshape(T, D), w_gate, w_up, w_down, layer=0)
    h, xn, wqkv16 = _ffn_down(hmid, wd16, h.reshape(T, D), norm_mix, 1, emit_residual=True, side_w=w_qkv)

    lambda_init = 0.8 - 0.6 * math.exp(-0.3 * 1)
    colscale = jnp.concatenate(
        [jnp.full((1, D), DIFF_HEAD_DIM ** -0.5 * LOG2E, _f32), jnp.ones((1, 2 * D), _f32)], axis=1)
    qkv, wo16 = _qkv_proj(xn, wqkv16, colscale, w_o)
    slopes = jnp.asarray([2.0 ** (-8.0 * (i + 1) / H) for i in range(H)], dtype=_f32)
    o = _diff_attention(qkv, slopes, lambda_q1, lambda_k1, lambda_q2, lambda_k2, subln_g,
                        B=B, S=S, H=H, lambda_init=lambda_init, attn_layer=0)
    h, xn = _wo_proj(o, wo16, h, norm_ffn, layer=1)
    hmid, wd16 = _ffn_up(xn, w_gate, w_up, w_down, layer=1)
    (out,) = _ffn_down(hmid, wd16, h, final_norm.reshape(1, D), 0, emit_residual=False)
    return out.reshape(B, S, D)
```

```python
import functools
import math
import struct

import jax
import jax.numpy as jnp
from jax import lax
from jax.experimental import pallas as pl
from jax.experimental.pallas import tpu as pltpu

RMS_EPS = 1e-6
POOL_WINDOWS = (2, 4, 8, 16)
POOL_HALO = 16
DIFF_HEAD_DIM = 128
V_HEAD_DIM = 2 * DIFF_HEAD_DIM
LANES = 128
ROW_GROUP = 64
VMEM_LIMIT_BYTES = 127 * 512 * 1024

_f32 = jnp.float32
_bf16 = jnp.bfloat16


def _params(n_axes, flags=None):
    return pltpu.CompilerParams(
        dimension_semantics=("arbitrary",) * n_axes,
        vmem_limit_bytes=VMEM_LIMIT_BYTES,
        flags=flags,
    )


def _rows(a):
    return a.reshape(a.shape[0], 1, a.shape[1])


def _rms(x, g):
    ms = jnp.mean(x * x, axis=-1, keepdims=True)
    return x * lax.rsqrt(ms + RMS_EPS) * g


def _pool_kernel(x_ref, halo_ref, gmix_ref, w_ref, scale_ref, gffn_ref, h_ref, xn_ref, *, ts, group):
    s = pl.program_id(1)
    x = x_ref[0]
    g = gmix_ref[...]
    hn = _rms(x, g)
    halo = jnp.where(s > 0, _rms(halo_ref[0], g), 0.0)
    ext = jnp.concatenate([halo, hn], axis=0)
    t = s * ts + lax.broadcasted_iota(jnp.int32, (ts, 1), 0)
    ssq = jnp.zeros((ts, 1), _f32)
    for gi, w in enumerate(POOL_WINDOWS):
        cols = slice(gi * group, (gi + 1) * group)
        acc = ext[:, cols]
        span = 1
        while span < w:
            acc = acc + pltpu.roll(acc, span, axis=0)
            span *= 2
        cnt = jnp.minimum(t + 1, w).astype(_f32)
        pooled = acc[POOL_HALO:, :] / cnt
        diff = (pooled - hn[:, cols]).astype(_bf16)
        y = jnp.dot(diff, w_ref[gi].astype(_bf16), preferred_element_type=_f32)
        hg = x[:, cols] + y * scale_ref[:, cols]
        h_ref[0, :, cols] = hg
        ssq = ssq + jnp.sum(hg * hg, axis=-1, keepdims=True)
    d_model = x.shape[-1]
    inv = lax.rsqrt(ssq / d_model + RMS_EPS)
    xn_ref[0] = (h_ref[0] * inv * gffn_ref[...]).astype(_bf16)


def _pool_layer(x, norm_mix, pool_w, pool_scale, norm_ffn, *, layer, pool_layer, ts=512):
    B, S, D = x.shape
    _, G, C, _ = pool_w.shape
    halo_blocks = ts // POOL_HALO
    kern = functools.partial(_pool_kernel, ts=ts, group=C)
    return pl.pallas_call(
        kern,
        grid=(B, S // ts),
        in_specs=[
            pl.BlockSpec((1, ts, D), lambda b, s: (b, s, 0)),
            pl.BlockSpec((1, POOL_HALO, D), lambda b, s: (b, jnp.maximum(s * halo_blocks - 1, 0), 0)),
            pl.BlockSpec((None, 1, D), lambda b, s: (layer, 0, 0)),
            pl.BlockSpec((None, G, C, C), lambda b, s: (pool_layer, 0, 0, 0)),
            pl.BlockSpec((None, 1, D), lambda b, s: (pool_layer, 0, 0)),
            pl.BlockSpec((None, 1, D), lambda b, s: (layer, 0, 0)),
        ],
        out_specs=[
            pl.BlockSpec((1, ts, D), lambda b, s: (b, s, 0)),
            pl.BlockSpec((1, ts, D), lambda b, s: (b, s, 0)),
        ],
        out_shape=[
            jax.ShapeDtypeStruct((B, S, D), _f32),
            jax.ShapeDtypeStruct((B, S, D), _bf16),
        ],
        compiler_params=_params(2),
        name="pool_mixer",
    )(x, x, _rows(norm_mix), pool_w, _rows(pool_scale), _rows(norm_ffn))


def _ffn_up_kernel(xn_ref, wg_ref, wu_ref, wd_ref, o_ref, wd16_ref):
    xn = xn_ref[...]
    gate = jnp.dot(xn, wg_ref[...].astype(_bf16), preferred_element_type=_f32)
    up = jnp.dot(xn, wu_ref[...].astype(_bf16), preferred_element_type=_f32)
    o_ref[...] = (gate / (1.0 + jnp.exp(-gate)) * up).astype(o_ref.dtype)
    wd16_ref[...] = wd_ref[...].astype(_bf16)


def _ffn_up(xn, w_gate, w_up, w_down, *, layer, tm=2048, tn=512):
    T, D = xn.shape
    F = w_gate.shape[2]
    ni, nj = T // tm, F // tn
    slab = F // (ni * nj)
    assert slab * ni * nj == F and slab % 16 == 0
    return pl.pallas_call(
        _ffn_up_kernel,
        grid=(ni, nj),
        in_specs=[
            pl.BlockSpec((tm, D), lambda i, j: (i, 0)),
            pl.BlockSpec((None, D, tn), lambda i, j: (layer, 0, j)),
            pl.BlockSpec((None, D, tn), lambda i, j: (layer, 0, j)),
            pl.BlockSpec((None, slab, D), lambda i, j: (layer, i * nj + j, 0)),
        ],
        out_specs=[
            pl.BlockSpec((tm, tn), lambda i, j: (i, j)),
            pl.BlockSpec((slab, D), lambda i, j: (i * nj + j, 0)),
        ],
        out_shape=[
            jax.ShapeDtypeStruct((T, F), _bf16),
            jax.ShapeDtypeStruct((F, D), _bf16),
        ],
        compiler_params=_params(2),
        name="ffn_up",
    )(xn, w_gate, w_up, w_down)


def _ffn_down_kernel(hm_ref, wd_ref, res_ref, g_ref, *refs, emit_residual, n_side):
    side_in, outs = refs[:n_side], refs[n_side:]
    h = res_ref[...] + jnp.dot(hm_ref[...], wd_ref[...], preferred_element_type=_f32)
    if emit_residual:
        outs[0][...] = h
        outs[1][...] = _rms(h, g_ref[...]).astype(_bf16)
    else:
        outs[0][...] = _rms(h, g_ref[...])
    for src, dst in zip(side_in, outs[len(outs) - n_side:]):
        dst[...] = src[...].astype(_bf16)


def _ffn_down(hmid, wd16, res, g, g_row, *, emit_residual, side_w=None, tm=256):
    T, F = hmid.shape
    D = wd16.shape[1]
    n = T // tm
    in_specs = [
        pl.BlockSpec((tm, F), lambda i: (i, 0)),
        pl.BlockSpec((F, D), lambda i: (0, 0), pipeline_mode=pl.Buffered(1)),
        pl.BlockSpec((tm, D), lambda i: (i, 0)),
        pl.BlockSpec((None, 1, D), lambda i: (g_row, 0, 0)),
    ]
    row_spec = pl.BlockSpec((tm, D), lambda i: (i, 0))
    if emit_residual:
        out_specs = [row_spec, row_spec]
        out_shape = [jax.ShapeDtypeStruct((T, D), _f32), jax.ShapeDtypeStruct((T, D), _bf16)]
    else:
        out_specs = [row_spec]
        out_shape = [jax.ShapeDtypeStruct((T, D), _f32)]
    args = [hmid, wd16, res, _rows(g)]
    if side_w is not None:
        _, R, C = side_w.shape
        slab = R // n
        assert slab * n == R and slab % 16 == 0
        in_specs.append(pl.BlockSpec((None, slab, C), lambda i: (0, i, 0)))
        out_specs.append(pl.BlockSpec((slab, C), lambda i: (i, 0)))
        out_shape.append(jax.ShapeDtypeStruct((R, C), _bf16))
        args.append(side_w)
    kern = functools.partial(_ffn_down_kernel, emit_residual=emit_residual, n_side=int(side_w is not None))
    return pl.pallas_call(
        kern,
        grid=(n,),
        in_specs=in_specs,
        out_specs=out_specs,
        out_shape=out_shape,
        compiler_params=_params(1),
        name="ffn_down",
    )(*args)


def _qkv_kernel(xn_ref, w_ref, cs_ref, wo_ref, o_ref, wo16_ref):
    acc = jnp.dot(xn_ref[...], w_ref[...], preferred_element_type=_f32)
    o_ref[...] = (acc * cs_ref[...]).astype(o_ref.dtype)
    wo16_ref[...] = wo_ref[...].astype(_bf16)


def _qkv_proj(xn, w16, colscale, w_o, *, tm=1024, tn=1536):
    T, D = xn.shape
    N = w16.shape[1]
    ni, nj = T // tm, N // tn
    _, R, C = w_o.shape
    slab = R // (ni * nj)
    assert slab * ni * nj == R and slab % 16 == 0
    return pl.pallas_call(
        _qkv_kernel,
        grid=(ni, nj),
        in_specs=[
            pl.BlockSpec((tm, D), lambda i, j: (i, 0)),
            pl.BlockSpec((D, tn), lambda i, j: (0, j)),
            pl.BlockSpec((1, tn), lambda i, j: (0, j)),
            pl.BlockSpec((None, slab, C), lambda i, j: (0, i * nj + j, 0)),
        ],
        out_specs=[
            pl.BlockSpec((tm, tn), lambda i, j: (i, j)),
            pl.BlockSpec((slab, C), lambda i, j: (i * nj + j, 0)),
        ],
        out_shape=[
            jax.ShapeDtypeStruct((T, N), _bf16),
            jax.ShapeDtypeStruct((R, C), _bf16),
        ],
        compiler_params=_params(2),
        name="qkv_proj",
    )(xn, w16, colscale, w_o)


def _bf16_round(x):
    bits = struct.unpack("<I", struct.pack("<f", x))[0]
    bits = (bits + 0x7FFF + ((bits >> 16) & 1)) & 0xFFFF0000
    return struct.unpack("<f", struct.pack("<I", bits))[0]


def _bf16_split(x, n):
    terms = []
    for _ in range(n):
        t = _bf16_round(x)
        terms.append(t)
        x -= t
    return terms


LOG2E = math.log2(math.e)
LOG2E_TERMS = _bf16_split(LOG2E, 4)
KPOS_RADIX_BITS = 6


def _attn_kernel(slopes_ref, q_ref, k_ref, v_ref, lq1_ref, lk1_ref, lq2_ref, lk2_ref, sg_ref, o_ref,
                 kaug_ref, qaug_ref, s0_ref, s1_ref, p0_ref, p1_ref, mrun_ref, m_ref, lrun_ref, l_ref, acc_ref,
                 *, tq, lambda_init, unroll):
    d = DIFF_HEAD_DIM
    S = q_ref.shape[0]
    nq = S // tq
    nt = len(LOG2E_TERMS)
    slope = slopes_ref[pl.program_id(1)]
    s_refs, p_refs = (s0_ref, s1_ref), (p0_ref, p1_ref)

    pos = lax.broadcasted_iota(jnp.int32, (S, d), 0)
    lane = lax.broadcasted_iota(jnp.int32, (S, d), 1)
    lo = (pos & ((1 << KPOS_RADIX_BITS) - 1)).astype(_f32)
    hi = (pos >> KPOS_RADIX_BITS).astype(_f32) * float(1 << KPOS_RADIX_BITS)
    kaug_ref[...] = (slope * jnp.where(lane < nt, lo, jnp.where(lane < 2 * nt, hi, 0.0))).astype(_bf16)
    qlane = lax.broadcasted_iota(jnp.int32, (tq, d), 1)
    qaug = jnp.zeros((tq, d), _f32)
    for i, term in enumerate(LOG2E_TERMS):
        qaug = jnp.where((qlane == i) | (qlane == nt + i), term, qaug)
    qaug_ref[...] = qaug.astype(_bf16)

    lam = (jnp.exp(jnp.sum(lq1_ref[...] * lk1_ref[...], keepdims=True))
           - jnp.exp(jnp.sum(lq2_ref[...] * lk2_ref[...], keepdims=True))
           + lambda_init)

    def rows(j):
        if isinstance(j, int):
            return pl.ds(j * tq, tq)
        return pl.ds(pl.multiple_of(j * tq, tq), tq)

    def lane_blocks(x):
        return [x[:, c0:c0 + LANES] for c0 in range(0, x.shape[1], LANES)]

    def scores_chunk(u, j):
        qi, comp = divmod(u, 2)
        cs = slice(comp * d, (comp + 1) * d)
        qa = jnp.concatenate([q_ref[rows(qi), cs], qaug_ref[...]], axis=1)
        ka = jnp.concatenate([k_ref[rows(j), cs], kaug_ref[rows(j), :]], axis=1)
        s = lax.dot_general(qa, ka, (((1,), (1,)), ((), ())), preferred_element_type=_f32)
        if isinstance(j, int) and j == qi:
            r = lax.broadcasted_iota(jnp.int32, (tq, tq), 0)
            c = lax.broadcasted_iota(jnp.int32, (tq, tq), 1)
            s = jnp.where(c <= r, s, -jnp.inf)
        s_refs[comp][j] = s
        mrun_ref[comp] = functools.reduce(jnp.maximum, lane_blocks(s), mrun_ref[comp])

    def probs_chunk(u, j):
        comp = u % 2
        for r0 in range(0, tq, ROW_GROUP):
            rs = slice(r0, r0 + ROW_GROUP)
            m = m_ref[comp, rs, :]
            lsum = lrun_ref[comp, rs, :]
            for b, sb in enumerate(lane_blocks(s_refs[comp][j, rs, :])):
                pb = jnp.exp2(sb - m)
                lsum = lsum + pb
                p_refs[comp][j, rs, b * LANES:(b + 1) * LANES] = pb.astype(_bf16)
            lrun_ref[comp, rs, :] = lsum

    def pv_chunk(u, j):
        comp = u % 2
        acc_ref[comp] += jnp.dot(p_refs[comp][j], v_ref[rows(j), :], preferred_element_type=_f32)

    def start_scores(u):
        mrun_ref[u % 2] = jnp.full((tq, LANES), -jnp.inf, _f32)

    def finish_scores(u):
        comp = u % 2
        m_ref[comp] = jnp.broadcast_to(jnp.max(mrun_ref[comp], axis=-1, keepdims=True), (tq, LANES))
        lrun_ref[comp] = jnp.zeros((tq, LANES), _f32)

    def finish_probs(u):
        comp = u % 2
        l_ref[comp] = jnp.broadcast_to(jnp.sum(lrun_ref[comp], axis=-1, keepdims=True), (tq, LANES))
        acc_ref[comp] = jnp.zeros((tq, 2 * d), _f32)

    def finish_tile(qi):
        gain = sg_ref[...] * (1.0 - lambda_init)
        for r0 in range(0, tq, ROW_GROUP):
            rs = slice(r0, r0 + ROW_GROUP)
            inv = [1.0 / l_ref[comp, rs, :] for comp in range(2)]
            inv = [jnp.concatenate([x] * (2 * d // LANES), axis=1) for x in inv]
            o = acc_ref[0, rs, :] * inv[0] - lam * (acc_ref[1, rs, :] * inv[1])
            ms = jnp.mean(o * o, axis=-1, keepdims=True)
            on = o * lax.rsqrt(ms + RMS_EPS) * gain
            o_ref[pl.ds(qi * tq + r0, ROW_GROUP), :] = on.astype(o_ref.dtype)

    n_units = 2 * nq
    chunks = lambda u: u // 2 + 1 if 0 <= u < n_units else 0

    start_scores(0)
    for j in range(chunks(0)):
        scores_chunk(0, j)
    finish_scores(0)
    for u in range(n_units + 1):
        n_s, n_p, n_v = chunks(u + 1), chunks(u), chunks(u - 1)
        if n_s:
            start_scores(u + 1)
        n_common = min(n for n in (n_s, n_p, n_v) if n)
        if n_s:
            n_common = min(n_common, n_s - 1)

        def chunk_step(j, u=u, n_s=n_s, n_p=n_p, n_v=n_v):
            if n_s:
                scores_chunk(u + 1, j)
            if n_p:
                probs_chunk(u, j)
            if n_v:
                pv_chunk(u - 1, j)

        def body(j, carry, chunk_step=chunk_step):
            chunk_step(j)
            return carry

        if n_common:
            lax.fori_loop(0, n_common, body, 0, unroll=min(unroll, n_common))
        for j in range(n_common, max(n_s, n_p, n_v)):
            if j < n_s:
                scores_chunk(u + 1, j)
            if j < n_p:
                probs_chunk(u, j)
            if j < n_v:
                pv_chunk(u - 1, j)
        if n_v and (u - 1) % 2 == 1:
            finish_tile((u - 1) // 2)
        if n_s:
            finish_scores(u + 1)
        if n_p:
            finish_probs(u)


def _diff_attention(qkv, slopes, lq1, lk1, lq2, lk2, subln_g, *, B, S, H, lambda_init, attn_layer, tq=512, unroll=3):
    d, e = DIFF_HEAD_DIM, V_HEAD_DIM
    kern = functools.partial(_attn_kernel, tq=tq, lambda_init=lambda_init, unroll=unroll)
    return pl.pallas_call(
        kern,
        grid_spec=pltpu.PrefetchScalarGridSpec(
            num_scalar_prefetch=1,
            grid=(B, H),
            in_specs=[
                pl.BlockSpec((S, e), lambda b, h, sl: (b, h)),
                pl.BlockSpec((S, e), lambda b, h, sl: (b, H + h)),
                pl.BlockSpec((S, e), lambda b, h, sl: (b, 2 * H + h)),
                pl.BlockSpec((None, 1, d), lambda b, h, sl: (attn_layer, 0, 0)),
                pl.BlockSpec((None, 1, d), lambda b, h, sl: (attn_layer, 0, 0)),
                pl.BlockSpec((None, 1, d), lambda b, h, sl: (attn_layer, 0, 0)),
                pl.BlockSpec((None, 1, d), lambda b, h, sl: (attn_layer, 0, 0)),
                pl.BlockSpec((None, 1, e), lambda b, h, sl: (attn_layer, 0, 0)),
            ],
            out_specs=pl.BlockSpec((S, e), lambda b, h, sl: (b, h)),
            scratch_shapes=[
                pltpu.VMEM((S, d), _bf16),
                pltpu.VMEM((tq, d), _bf16),
                pltpu.VMEM((S // tq, tq, tq), _f32),
                pltpu.VMEM((S // tq, tq, tq), _f32),
                pltpu.VMEM((S // tq, tq, tq), _bf16),
                pltpu.VMEM((S // tq, tq, tq), _bf16),
                pltpu.VMEM((2, tq, LANES), _f32),
                pltpu.VMEM((2, tq, LANES), _f32),
                pltpu.VMEM((2, tq, LANES), _f32),
                pltpu.VMEM((2, tq, LANES), _f32),
                pltpu.VMEM((2, tq, e), _f32),
            ],
        ),
        out_shape=jax.ShapeDtypeStruct((B * S, H * e), _bf16),
        compiler_params=_params(2),
        name="diff_attention",
    )(slopes, qkv, qkv, qkv, _rows(lq1), _rows(lk1), _rows(lq2), _rows(lk2), _rows(subln_g))


def _wo_kernel(o_ref, w_ref, res_ref, g_ref, h_ref, xn_ref):
    h = res_ref[...] + jnp.dot(o_ref[...], w_ref[...], preferred_element_type=_f32)
    h_ref[...] = h
    xn_ref[...] = _rms(h, g_ref[...]).astype(_bf16)


def _wo_proj(o, w16, res, norm_ffn, *, layer, tm=512):
    T, D = res.shape
    return pl.pallas_call(
        _wo_kernel,
        grid=(T // tm,),
        in_specs=[
            pl.BlockSpec((tm, D), lambda i: (i, 0)),
            pl.BlockSpec((D, D), lambda i: (0, 0), pipeline_mode=pl.Buffered(1)),
            pl.BlockSpec((tm, D), lambda i: (i, 0)),
            pl.BlockSpec((None, 1, D), lambda i: (layer, 0, 0)),
        ],
        out_specs=[
            pl.BlockSpec((tm, D), lambda i: (i, 0)),
            pl.BlockSpec((tm, D), lambda i: (i, 0)),
        ],
        out_shape=[
            jax.ShapeDtypeStruct((T, D), _f32),
            jax.ShapeDtypeStruct((T, D), _bf16),
        ],
        compiler_params=_params(1),
        name="wo_proj",
    )(o, w16, res, _rows(norm_ffn))


def kernel(x, norm_mix, norm_ffn, pool_w, pool_scale, w_qkv, lambda_q1, lambda_k1, lambda_q2, lambda_k2,
           subln_g, w_o, w_gate, w_up, w_down, final_norm):
    B, S, D = x.shape
    T = B * S
    H = D // V_HEAD_DIM

    h, xn = _pool_layer(x, norm_mix, pool_w, pool_scale, norm_ffn, layer=0, pool_layer=0)
    hmid, wd16 = _ffn_up(xn.reshape(T, D), w_gate, w_up, w_down, layer=0)
    h, xn, wqkv16 = _ffn_down(hmid, wd16, h.reshape(T, D), norm_mix, 1, emit_residual=True, side_w=w_qkv)

    lambda_init = 0.8 - 0.6 * math.exp(-0.3 * 1)
    colscale = jnp.concatenate(
        [jnp.full((1, D), DIFF_HEAD_DIM ** -0.5 * LOG2E, _f32), jnp.ones((1, 2 * D), _f32)], axis=1)
    qkv, wo16 = _qkv_proj(xn, wqkv16, colscale, w_o)
    slopes = jnp.asarray([2.0 ** (-8.0 * (i + 1) / H) for i in range(H)], dtype=_f32)
    o = _diff_attention(qkv, slopes, lambda_q1, lambda_k1, lambda_q2, lambda_k2, subln_g,
                        B=B, S=S, H=H, lambda_init=lambda_init, attn_layer=0)
    h, xn = _wo_proj(o, wo16, h, norm_ffn, layer=1)
    hmid, wd16 = _ffn_up(xn, w_gate, w_up, w_down, layer=1)
    (out,) = _ffn_down(hmid, wd16, h, final_norm.reshape(1, D), 0, emit_residual=False)
    return out.reshape(B, S, D)
```

```python
import functools
import math
import struct

import jax
import jax.numpy as jnp
from jax import lax
from jax.experimental import pallas as pl
from jax.experimental.pallas import tpu as pltpu

RMS_EPS = 1e-6
POOL_WINDOWS = (2, 4, 8, 16)
POOL_HALO = 16
DIFF_HEAD_DIM = 128
V_HEAD_DIM = 2 * DIFF_HEAD_DIM
LANES = 128
ROW_GROUP = 64
VMEM_LIMIT_BYTES = 127 * 512 * 1024

_f32 = jnp.float32
_bf16 = jnp.bfloat16


def _params(n_axes, flags=None):
    return pltpu.CompilerParams(
        dimension_semantics=("arbitrary",) * n_axes,
        vmem_limit_bytes=VMEM_LIMIT_BYTES,
        flags=flags,
    )


def _rows(a):
    return a.reshape(a.shape[0], 1, a.shape[1])


def _rms(x, g):
    ms = jnp.mean(x * x, axis=-1, keepdims=True)
    return x * lax.rsqrt(ms + RMS_EPS) * g


def _pool_kernel(x_ref, halo_ref, gmix_ref, w_ref, scale_ref, gffn_ref, h_ref, xn_ref, *, ts, group):
    s = pl.program_id(1)
    x = x_ref[0]
    g = gmix_ref[...]
    hn = _rms(x, g)
    halo = jnp.where(s > 0, _rms(halo_ref[0], g), 0.0)
    ext = jnp.concatenate([halo, hn], axis=0)
    t = s * ts + lax.broadcasted_iota(jnp.int32, (ts, 1), 0)
    ssq = jnp.zeros((ts, 1), _f32)
    for gi, w in enumerate(POOL_WINDOWS):
        cols = slice(gi * group, (gi + 1) * group)
        acc = ext[:, cols]
        span = 1
        while span < w:
            acc = acc + pltpu.roll(acc, span, axis=0)
            span *= 2
        cnt = jnp.minimum(t + 1, w).astype(_f32)
        pooled = acc[POOL_HALO:, :] / cnt
        diff = (pooled - hn[:, cols]).astype(_bf16)
        y = jnp.dot(diff, w_ref[gi].astype(_bf16), preferred_element_type=_f32)
        hg = x[:, cols] + y * scale_ref[:, cols]
        h_ref[0, :, cols] = hg
        ssq = ssq + jnp.sum(hg * hg, axis=-1, keepdims=True)
    d_model = x.shape[-1]
    inv = lax.rsqrt(ssq / d_model + RMS_EPS)
    xn_ref[0] = (h_ref[0] * inv * gffn_ref[...]).astype(_bf16)


def _pool_layer(x, norm_mix, pool_w, pool_scale, norm_ffn, *, layer, pool_layer, ts=512):
    B, S, D = x.shape
    _, G, C, _ = pool_w.shape
    halo_blocks = ts // POOL_HALO
    kern = functools.partial(_pool_kernel, ts=ts, group=C)
    return pl.pallas_call(
        kern,
        grid=(B, S // ts),
        in_specs=[
            pl.BlockSpec((1, ts, D), lambda b, s: (b, s, 0)),
            pl.BlockSpec((1, POOL_HALO, D), lambda b, s: (b, jnp.maximum(s * halo_blocks - 1, 0), 0)),
            pl.BlockSpec((None, 1, D), lambda b, s: (layer, 0, 0)),
            pl.BlockSpec((None, G, C, C), lambda b, s: (pool_layer, 0, 0, 0)),
            pl.BlockSpec((None, 1, D), lambda b, s: (pool_layer, 0, 0)),
            pl.BlockSpec((None, 1, D), lambda b, s: (layer, 0, 0)),
        ],
        out_specs=[
            pl.BlockSpec((1, ts, D), lambda b, s: (b, s, 0)),
            pl.BlockSpec((1, ts, D), lambda b, s: (b, s, 0)),
        ],
        out_shape=[
            jax.ShapeDtypeStruct((B, S, D), _f32),
            jax.ShapeDtypeStruct((B, S, D), _bf16),
        ],
        compiler_params=_params(2),
        name="pool_mixer",
    )(x, x, _rows(norm_mix), pool_w, _rows(pool_scale), _rows(norm_ffn))


def _ffn_up_kernel(xn_ref, wg_ref, wu_ref, wd_ref, o_ref, wd16_ref):
    xn = xn_ref[...]
    gate = jnp.dot(xn, wg_ref[...].astype(_bf16), preferred_element_type=_f32)
    up = jnp.dot(xn, wu_ref[...].astype(_bf16), preferred_element_type=_f32)
    o_ref[...] = (gate / (1.0 + jnp.exp(-gate)) * up).astype(o_ref.dtype)
    wd16_ref[...] = wd_ref[...].astype(_bf16)


def _ffn_up(xn, w_gate, w_up, w_down, *, layer, tm=2048, tn=512):
    T, D = xn.shape
    F = w_gate.shape[2]
    ni, nj = T // tm, F // tn
    slab = F // (ni * nj)
    assert slab * ni * nj == F and slab % 16 == 0
    return pl.pallas_call(
        _ffn_up_kernel,
        grid=(ni, nj),
        in_specs=[
            pl.BlockSpec((tm, D), lambda i, j: (i, 0)),
            pl.BlockSpec((None, D, tn), lambda i, j: (layer, 0, j)),
            pl.BlockSpec((None, D, tn), lambda i, j: (layer, 0, j)),
            pl.BlockSpec((None, slab, D), lambda i, j: (layer, i * nj + j, 0)),
        ],
        out_specs=[
            pl.BlockSpec((tm, tn), lambda i, j: (i, j)),
            pl.BlockSpec((slab, D), lambda i, j: (i * nj + j, 0)),
        ],
        out_shape=[
            jax.ShapeDtypeStruct((T, F), _bf16),
            jax.ShapeDtypeStruct((F, D), _bf16),
        ],
        compiler_params=_params(2),
        name="ffn_up",
    )(xn, w_gate, w_up, w_down)


def _ffn_down_kernel(hm_ref, wd_ref, res_ref, g_ref, *refs, emit_residual, n_side):
    side_in, outs = refs[:n_side], refs[n_side:]
    h = res_ref[...] + jnp.dot(hm_ref[...], wd_ref[...], preferred_element_type=_f32)
    if emit_residual:
        outs[0][...] = h
        outs[1][...] = _rms(h, g_ref[...]).astype(_bf16)
    else:
        outs[0][...] = _rms(h, g_ref[...])
    for src, dst in zip(side_in, outs[len(outs) - n_side:]):
        dst[...] = src[...].astype(_bf16)


def _ffn_down(hmid, wd16, res, g, g_row, *, emit_residual, side_w=None, tm=256):
    T, F = hmid.shape
    D = wd16.shape[1]
    n = T // tm
    in_specs = [
        pl.BlockSpec((tm, F), lambda i: (i, 0)),
        pl.BlockSpec((F, D), lambda i: (0, 0), pipeline_mode=pl.Buffered(1)),
        pl.BlockSpec((tm, D), lambda i: (i, 0)),
        pl.BlockSpec((None, 1, D), lambda i: (g_row, 0, 0)),
    ]
    row_spec = pl.BlockSpec((tm, D), lambda i: (i, 0))
    if emit_residual:
        out_specs = [row_spec, row_spec]
        out_shape = [jax.ShapeDtypeStruct((T, D), _f32), jax.ShapeDtypeStruct((T, D), _bf16)]
    else:
        out_specs = [row_spec]
        out_shape = [jax.ShapeDtypeStruct((T, D), _f32)]
    args = [hmid, wd16, res, _rows(g)]
    if side_w is not None:
        _, R, C = side_w.shape
        slab = R // n
        assert slab * n == R and slab % 16 == 0
        in_specs.append(pl.BlockSpec((None, slab, C), lambda i: (0, i, 0)))
        out_specs.append(pl.BlockSpec((slab, C), lambda i: (i, 0)))
        out_shape.append(jax.ShapeDtypeStruct((R, C), _bf16))
        args.append(side_w)
    kern = functools.partial(_ffn_down_kernel, emit_residual=emit_residual, n_side=int(side_w is not None))
    return pl.pallas_call(
        kern,
        grid=(n,),
        in_specs=in_specs,
        out_specs=out_specs,
        out_shape=out_shape,
        compiler_params=_params(1),
        name="ffn_down",
    )(*args)


def _qkv_kernel(xn_ref, w_ref, cs_ref, wo_ref, o_ref, wo16_ref):
    acc = jnp.dot(xn_ref[...], w_ref[...], preferred_element_type=_f32)
    o_ref[...] = (acc * cs_ref[...]).astype(o_ref.dtype)
    wo16_ref[...] = wo_ref[...].astype(_bf16)


def _qkv_proj(xn, w16, colscale, w_o, *, tm=1024, tn=1536):
    T, D = xn.shape
    N = w16.shape[1]
    ni, nj = T // tm, N // tn
    _, R, C = w_o.shape
    slab = R // (ni * nj)
    assert slab * ni * nj == R and slab % 16 == 0
    return pl.pallas_call(
        _qkv_kernel,
        grid=(ni, nj),
        in_specs=[
            pl.BlockSpec((tm, D), lambda i, j: (i, 0)),
            pl.BlockSpec((D, tn), lambda i, j: (0, j)),
            pl.BlockSpec((1, tn), lambda i, j: (0, j)),
            pl.BlockSpec((None, slab, C), lambda i, j: (0, i * nj + j, 0)),
        ],
        out_specs=[
            pl.BlockSpec((tm, tn), lambda i, j: (i, j)),
            pl.BlockSpec((slab, C), lambda i, j: (i * nj + j, 0)),
        ],
        out_shape=[
            jax.ShapeDtypeStruct((T, N), _bf16),
            jax.ShapeDtypeStruct((R, C), _bf16),
        ],
        compiler_params=_params(2),
        name="qkv_proj",
    )(xn, w16, colscale, w_o)


def _bf16_round(x):
    bits = struct.unpack("<I", struct.pack("<f", x))[0]
    bits = (bits + 0x7FFF + ((bits >> 16) & 1)) & 0xFFFF0000
    return struct.unpack("<f", struct.pack("<I", bits))[0]


def _bf16_split(x, n):
    terms = []
    for _ in range(n):
        t = _bf16_round(x)
        terms.append(t)
        x -= t
    return terms


LOG2E = math.log2(math.e)
LOG2E_TERMS = _bf16_split(LOG2E, 4)
KPOS_RADIX_BITS = 6


def _attn_kernel(slopes_ref, q_ref, k_ref, v_ref, lq1_ref, lk1_ref, lq2_ref, lk2_ref, sg_ref, o_ref,
                 kaug_ref, qaug_ref, s0_ref, s1_ref, mrun_ref, m_ref, lrun_ref, l_ref, acc_ref,
                 *, tq, lambda_init, unroll):
    d = DIFF_HEAD_DIM
    S = q_ref.shape[0]
    nq = S // tq
    nt = len(LOG2E_TERMS)
    slope = slopes_ref[pl.program_id(1)]
    s_refs = (s0_ref, s1_ref)

    pos = lax.broadcasted_iota(jnp.int32, (S, d), 0)
    lane = lax.broadcasted_iota(jnp.int32, (S, d), 1)
    lo = (pos & ((1 << KPOS_RADIX_BITS) - 1)).astype(_f32)
    hi = (pos >> KPOS_RADIX_BITS).astype(_f32) * float(1 << KPOS_RADIX_BITS)
    kaug_ref[...] = (slope * jnp.where(lane < nt, lo, jnp.where(lane < 2 * nt, hi, 0.0))).astype(_bf16)
    qlane = lax.broadcasted_iota(jnp.int32, (tq, d), 1)
    qaug = jnp.zeros((tq, d), _f32)
    for i, term in enumerate(LOG2E_TERMS):
        qaug = jnp.where((qlane == i) | (qlane == nt + i), term, qaug)
    qaug_ref[...] = qaug.astype(_bf16)

    lam = (jnp.exp(jnp.sum(lq1_ref[...] * lk1_ref[...], keepdims=True))
           - jnp.exp(jnp.sum(lq2_ref[...] * lk2_ref[...], keepdims=True))
           + lambda_init)

    def rows(j):
        if isinstance(j, int):
            return pl.ds(j * tq, tq)
        return pl.ds(pl.multiple_of(j * tq, tq), tq)

    def lane_blocks(x):
        return [x[:, c0:c0 + LANES] for c0 in range(0, x.shape[1], LANES)]

    def scores_chunk(u, j):
        qi, comp = divmod(u, 2)
        cs = slice(comp * d, (comp + 1) * d)
        qa = jnp.concatenate([q_ref[rows(qi), cs], qaug_ref[...]], axis=1)
        ka = jnp.concatenate([k_ref[rows(j), cs], kaug_ref[rows(j), :]], axis=1)
        s = lax.dot_general(qa, ka, (((1,), (1,)), ((), ())), preferred_element_type=_f32)
        if isinstance(j, int) and j == qi:
            r = lax.broadcasted_iota(jnp.int32, (tq, tq), 0)
            c = lax.broadcasted_iota(jnp.int32, (tq, tq), 1)
            s = jnp.where(c <= r, s, -jnp.inf)
        s_refs[comp][j] = s
        mrun_ref[comp] = functools.reduce(jnp.maximum, lane_blocks(s), mrun_ref[comp])

    def probs_pv_chunk(u, j):
        comp = u % 2
        p_rows = []
        for r0 in range(0, tq, ROW_GROUP):
            rs = slice(r0, r0 + ROW_GROUP)
            m = m_ref[comp, rs, :]
            lsum = lrun_ref[comp, rs, :]
            p_blocks = []
            for sb in lane_blocks(s_refs[comp][j, rs, :]):
                pb = jnp.exp2(sb - m)
                lsum = lsum + pb
                p_blocks.append(pb.astype(_bf16))
            lrun_ref[comp, rs, :] = lsum
            p_rows.append(jnp.concatenate(p_blocks, axis=1))
        p = jnp.concatenate(p_rows, axis=0)
        acc_ref[comp] += jnp.dot(p, v_ref[rows(j), :], preferred_element_type=_f32)

    def start_scores(u):
        mrun_ref[u % 2] = jnp.full((tq, LANES), -jnp.inf, _f32)

    def finish_scores(u):
        comp = u % 2
        m_ref[comp] = jnp.broadcast_to(jnp.max(mrun_ref[comp], axis=-1, keepdims=True), (tq, LANES))
        lrun_ref[comp] = jnp.zeros((tq, LANES), _f32)
        acc_ref[comp] = jnp.zeros((tq, 2 * d), _f32)

    def finish_probs(u):
        comp = u % 2
        l_ref[comp] = jnp.broadcast_to(jnp.sum(lrun_ref[comp], axis=-1, keepdims=True), (tq, LANES))

    def finish_tile(qi):
        gain = sg_ref[...] * (1.0 - lambda_init)
        for r0 in range(0, tq, ROW_GROUP):
            rs = slice(r0, r0 + ROW_GROUP)
            inv = [1.0 / l_ref[comp, rs, :] for comp in range(2)]
            inv = [jnp.concatenate([x] * (2 * d // LANES), axis=1) for x in inv]
            o = acc_ref[0, rs, :] * inv[0] - lam * (acc_ref[1, rs, :] * inv[1])
            ms = jnp.mean(o * o, axis=-1, keepdims=True)
            on = o * lax.rsqrt(ms + RMS_EPS) * gain
            o_ref[pl.ds(qi * tq + r0, ROW_GROUP), :] = on.astype(o_ref.dtype)

    n_units = 2 * nq
    chunks = lambda u: u // 2 + 1 if 0 <= u < n_units else 0

    start_scores(0)
    for j in range(chunks(0)):
        scores_chunk(0, j)
    finish_scores(0)
    for u in range(n_units):
        n_s, n_p = chunks(u + 1), chunks(u)
        if n_s:
            start_scores(u + 1)
        n_common = min(n_p, n_s - 1) if n_s else n_p

        def chunk_step(j, u=u, n_s=n_s):
            if n_s:
                scores_chunk(u + 1, j)
            probs_pv_chunk(u, j)

        def body(j, carry, chunk_step=chunk_step):
            chunk_step(j)
            return carry

        if n_common:
            lax.fori_loop(0, n_common, body, 0, unroll=min(unroll, n_common))
        for j in range(n_common, max(n_s, n_p)):
            if j < n_s:
                scores_chunk(u + 1, j)
            if j < n_p:
                probs_pv_chunk(u, j)
        finish_probs(u)
        if u % 2 == 1:
            finish_tile(u // 2)
        if n_s:
            finish_scores(u + 1)


def _diff_attention(qkv, slopes, lq1, lk1, lq2, lk2, subln_g, *, B, S, H, lambda_init, attn_layer, tq=512, unroll=3):
    d, e = DIFF_HEAD_DIM, V_HEAD_DIM
    kern = functools.partial(_attn_kernel, tq=tq, lambda_init=lambda_init, unroll=unroll)
    return pl.pallas_call(
        kern,
        grid_spec=pltpu.PrefetchScalarGridSpec(
            num_scalar_prefetch=1,
            grid=(B, H),
            in_specs=[
                pl.BlockSpec((S, e), lambda b, h, sl: (b, h)),
                pl.BlockSpec((S, e), lambda b, h, sl: (b, H + h)),
                pl.BlockSpec((S, e), lambda b, h, sl: (b, 2 * H + h)),
                pl.BlockSpec((None, 1, d), lambda b, h, sl: (attn_layer, 0, 0)),
                pl.BlockSpec((None, 1, d), lambda b, h, sl: (attn_layer, 0, 0)),
                pl.BlockSpec((None, 1, d), lambda b, h, sl: (attn_layer, 0, 0)),
                pl.BlockSpec((None, 1, d), lambda b, h, sl: (attn_layer, 0, 0)),
                pl.BlockSpec((None, 1, e), lambda b, h, sl: (attn_layer, 0, 0)),
            ],
            out_specs=pl.BlockSpec((S, e), lambda b, h, sl: (b, h)),
            scratch_shapes=[
                pltpu.VMEM((S, d), _bf16),
                pltpu.VMEM((tq, d), _bf16),
                pltpu.VMEM((S // tq, tq, tq), _f32),
                pltpu.VMEM((S // tq, tq, tq), _f32),
                pltpu.VMEM((2, tq, LANES), _f32),
                pltpu.VMEM((2, tq, LANES), _f32),
                pltpu.VMEM((2, tq, LANES), _f32),
                pltpu.VMEM((2, tq, LANES), _f32),
                pltpu.VMEM((2, tq, e), _f32),
            ],
        ),
        out_shape=jax.ShapeDtypeStruct((B * S, H * e), _bf16),
        compiler_params=_params(2),
        name="diff_attention",
    )(slopes, qkv, qkv, qkv, _rows(lq1), _rows(lk1), _rows(lq2), _rows(lk2), _rows(subln_g))


def _wo_kernel(o_ref, w_ref, res_ref, g_ref, h_ref, xn_ref):
    h = res_ref[...] + jnp.dot(o_ref[...], w_ref[...], preferred_element_type=_f32)
    h_ref[...] = h
    xn_ref[...] = _rms(h, g_ref[...]).astype(_bf16)


def _wo_proj(o, w16, res, norm_ffn, *, layer, tm=512):
    T, D = res.shape
    return pl.pallas_call(
        _wo_kernel,
        grid=(T // tm,),
        in_specs=[
            pl.BlockSpec((tm, D), lambda i: (i, 0)),
            pl.BlockSpec((D, D), lambda i: (0, 0), pipeline_mode=pl.Buffered(1)),
            pl.BlockSpec((tm, D), lambda i: (i, 0)),
            pl.BlockSpec((None, 1, D), lambda i: (layer, 0, 0)),
        ],
        out_specs=[
            pl.BlockSpec((tm, D), lambda i: (i, 0)),
            pl.BlockSpec((tm, D), lambda i: (i, 0)),
        ],
        out_shape=[
            jax.ShapeDtypeStruct((T, D), _f32),
            jax.ShapeDtypeStruct((T, D), _bf16),
        ],
        compiler_params=_params(1),
        name="wo_proj",
    )(o, w16, res, _rows(norm_ffn))


def kernel(x, norm_mix, norm_ffn, pool_w, pool_scale, w_qkv, lambda_q1, lambda_k1, lambda_q2, lambda_k2,
           subln_g, w_o, w_gate, w_up, w_down, final_norm):
    B, S, D = x.shape
    T = B * S
    H = D // V_HEAD_DIM

    h, xn = _pool_layer(x, norm_mix, pool_w, pool_scale, norm_ffn, layer=0, pool_layer=0)
    hmid, wd16 = _ffn_up(xn.reshape(T, D), w_gate, w_up, w_down, layer=0)
    h, xn, wqkv16 = _ffn_down(hmid, wd16, h.reshape(T, D), norm_mix, 1, emit_residual=True, side_w=w_qkv)

    lambda_init = 0.8 - 0.6 * math.exp(-0.3 * 1)
    colscale = jnp.concatenate(
        [jnp.full((1, D), DIFF_HEAD_DIM ** -0.5 * LOG2E, _f32), jnp.ones((1, 2 * D), _f32)], axis=1)
    qkv, wo16 = _qkv_proj(xn, wqkv16, colscale, w_o)
    slopes = jnp.asarray([2.0 ** (-8.0 * (i + 1) / H) for i in range(H)], dtype=_f32)
    o = _diff_attention(qkv, slopes, lambda_q1, lambda_k1, lambda_q2, lambda_k2, subln_g,
                        B=B, S=S, H=H, lambda_init=lambda_init, attn_layer=0)
    h, xn = _wo_proj(o, wo16, h, norm_ffn, layer=1)
    hmid, wd16 = _ffn_up(xn, w_gate, w_up, w_down, layer=1)
    (out,) = _ffn_down(hmid, wd16, h, final_norm.reshape(1, D), 0, emit_residual=False)
    return out.reshape(B, S, D)
```

```python
import functools
import math
import struct

import jax
import jax.numpy as jnp
from jax import lax
from jax.experimental import pallas as pl
from jax.experimental.pallas import tpu as pltpu

RMS_EPS = 1e-6
POOL_WINDOWS = (2, 4, 8, 16)
POOL_HALO = 16
DIFF_HEAD_DIM = 128
V_HEAD_DIM = 2 * DIFF_HEAD_DIM
LANES = 128
ROW_GROUP = 64
VMEM_LIMIT_BYTES = 127 * 512 * 1024

_f32 = jnp.float32
_bf16 = jnp.bfloat16


def _params(n_axes, flags=None):
    return pltpu.CompilerParams(
        dimension_semantics=("arbitrary",) * n_axes,
        vmem_limit_bytes=VMEM_LIMIT_BYTES,
        flags=flags,
    )


def _rows(a):
    return a.reshape(a.shape[0], 1, a.shape[1])


def _rms(x, g):
    ms = jnp.mean(x * x, axis=-1, keepdims=True)
    return x * lax.rsqrt(ms + RMS_EPS) * g


def _pool_kernel(x_ref, halo_ref, gmix_ref, w_ref, scale_ref, gffn_ref, h_ref, xn_ref, *, ts, group):
    s = pl.program_id(1)
    x = x_ref[0]
    g = gmix_ref[...]
    hn = _rms(x, g)
    halo = jnp.where(s > 0, _rms(halo_ref[0], g), 0.0)
    ext = jnp.concatenate([halo, hn], axis=0)
    t = s * ts + lax.broadcasted_iota(jnp.int32, (ts, 1), 0)
    ssq = jnp.zeros((ts, 1), _f32)
    for gi, w in enumerate(POOL_WINDOWS):
        cols = slice(gi * group, (gi + 1) * group)
        acc = ext[:, cols]
        span = 1
        while span < w:
            acc = acc + pltpu.roll(acc, span, axis=0)
            span *= 2
        cnt = jnp.minimum(t + 1, w).astype(_f32)
        pooled = acc[POOL_HALO:, :] / cnt
        diff = (pooled - hn[:, cols]).astype(_bf16)
        y = jnp.dot(diff, w_ref[gi].astype(_bf16), preferred_element_type=_f32)
        hg = x[:, cols] + y * scale_ref[:, cols]
        h_ref[0, :, cols] = hg
        ssq = ssq + jnp.sum(hg * hg, axis=-1, keepdims=True)
    d_model = x.shape[-1]
    inv = lax.rsqrt(ssq / d_model + RMS_EPS)
    xn_ref[0] = (h_ref[0] * inv * gffn_ref[...]).astype(_bf16)


def _pool_layer(x, norm_mix, pool_w, pool_scale, norm_ffn, *, layer, pool_layer, ts=512):
    B, S, D = x.shape
    _, G, C, _ = pool_w.shape
    halo_blocks = ts // POOL_HALO
    kern = functools.partial(_pool_kernel, ts=ts, group=C)
    return pl.pallas_call(
        kern,
        grid=(B, S // ts),
        in_specs=[
            pl.BlockSpec((1, ts, D), lambda b, s: (b, s, 0)),
            pl.BlockSpec((1, POOL_HALO, D), lambda b, s: (b, jnp.maximum(s * halo_blocks - 1, 0), 0)),
            pl.BlockSpec((None, 1, D), lambda b, s: (layer, 0, 0)),
            pl.BlockSpec((None, G, C, C), lambda b, s: (pool_layer, 0, 0, 0)),
            pl.BlockSpec((None, 1, D), lambda b, s: (pool_layer, 0, 0)),
            pl.BlockSpec((None, 1, D), lambda b, s: (layer, 0, 0)),
        ],
        out_specs=[
            pl.BlockSpec((1, ts, D), lambda b, s: (b, s, 0)),
            pl.BlockSpec((1, ts, D), lambda b, s: (b, s, 0)),
        ],
        out_shape=[
            jax.ShapeDtypeStruct((B, S, D), _f32),
            jax.ShapeDtypeStruct((B, S, D), _bf16),
        ],
        compiler_params=_params(2),
        name="pool_mixer",
    )(x, x, _rows(norm_mix), pool_w, _rows(pool_scale), _rows(norm_ffn))


def _ffn_up_kernel(xn_ref, wg_ref, wu_ref, *refs, n_side):
    side_in, o_ref, side_out = refs[:n_side], refs[n_side], refs[n_side + 1:]
    xn = xn_ref[...]
    gate = jnp.dot(xn, wg_ref[...].astype(_bf16), preferred_element_type=_f32)
    up = jnp.dot(xn, wu_ref[...].astype(_bf16), preferred_element_type=_f32)
    o_ref[...] = (gate / (1.0 + jnp.exp(-gate)) * up).astype(o_ref.dtype)
    for src, dst in zip(side_in, side_out):
        dst[...] = src[...].astype(_bf16)


def _side_cast_specs(w, layer, n_steps, step_of):
    _, R, C = w.shape
    bf16_rows = 16
    slab = next(s for s in range(bf16_rows, R + 1, bf16_rows) if R % s == 0 and R // s <= n_steps)
    last = R // slab - 1
    return (pl.BlockSpec((None, slab, C), lambda *g: (layer, jnp.minimum(step_of(*g), last), 0)),
            pl.BlockSpec((slab, C), lambda *g: (jnp.minimum(step_of(*g), last), 0)),
            jax.ShapeDtypeStruct((R, C), _bf16))


def _ffn_up(xn, w_gate, w_up, side_weights, *, layer, tm=2048, tn=512):
    T, D = xn.shape
    F = w_gate.shape[2]
    ni, nj = T // tm, F // tn
    side = [_side_cast_specs(w, l, ni * nj, lambda i, j: i * nj + j) for w, l in side_weights]
    return pl.pallas_call(
        functools.partial(_ffn_up_kernel, n_side=len(side)),
        grid=(ni, nj),
        in_specs=[
            pl.BlockSpec((tm, D), lambda i, j: (i, 0)),
            pl.BlockSpec((None, D, tn), lambda i, j: (layer, 0, j)),
            pl.BlockSpec((None, D, tn), lambda i, j: (layer, 0, j)),
        ] + [s[0] for s in side],
        out_specs=[pl.BlockSpec((tm, tn), lambda i, j: (i, j))] + [s[1] for s in side],
        out_shape=[jax.ShapeDtypeStruct((T, F), _bf16)] + [s[2] for s in side],
        compiler_params=_params(2),
        name="ffn_up",
    )(xn, w_gate, w_up, *[w for w, _ in side_weights])


def _ffn_down_kernel(hm_ref, wd_ref, res_ref, g_ref, *outs, emit_residual):
    h = res_ref[...] + jnp.dot(hm_ref[...], wd_ref[...], preferred_element_type=_f32)
    if emit_residual:
        outs[0][...] = h
        outs[1][...] = _rms(h, g_ref[...]).astype(_bf16)
    else:
        outs[0][...] = _rms(h, g_ref[...])


def _ffn_down(hmid, wd16, res, g, g_row, *, emit_residual, tm=512):
    T, F = hmid.shape
    D = wd16.shape[1]
    row_spec = pl.BlockSpec((tm, D), lambda i: (i, 0))
    if emit_residual:
        out_specs = [row_spec, row_spec]
        out_shape = [jax.ShapeDtypeStruct((T, D), _f32), jax.ShapeDtypeStruct((T, D), _bf16)]
    else:
        out_specs = [row_spec]
        out_shape = [jax.ShapeDtypeStruct((T, D), _f32)]
    return pl.pallas_call(
        functools.partial(_ffn_down_kernel, emit_residual=emit_residual),
        grid=(T // tm,),
        in_specs=[
            pl.BlockSpec((tm, F), lambda i: (i, 0)),
            pl.BlockSpec((F, D), lambda i: (0, 0), pipeline_mode=pl.Buffered(1)),
            row_spec,
            pl.BlockSpec((None, 1, D), lambda i: (g_row, 0, 0)),
        ],
        out_specs=out_specs,
        out_shape=out_shape,
        compiler_params=_params(1),
        name="ffn_down",
    )(hmid, wd16, res, _rows(g))


def _qkv_kernel(xn_ref, w_ref, cs_ref, wo_ref, o_ref, wo16_ref):
    acc = jnp.dot(xn_ref[...], w_ref[...], preferred_element_type=_f32)
    o_ref[...] = (acc * cs_ref[...]).astype(o_ref.dtype)
    wo16_ref[...] = wo_ref[...].astype(_bf16)


def _qkv_proj(xn, w16, colscale, w_o, *, attn_layer, tm=1024, tn=1536):
    T, D = xn.shape
    N = w16.shape[1]
    ni, nj = T // tm, N // tn
    side_in, side_out, side_shape = _side_cast_specs(w_o, attn_layer, ni * nj, lambda i, j: i * nj + j)
    return pl.pallas_call(
        _qkv_kernel,
        grid=(ni, nj),
        in_specs=[
            pl.BlockSpec((tm, D), lambda i, j: (i, 0)),
            pl.BlockSpec((D, tn), lambda i, j: (0, j)),
            pl.BlockSpec((1, tn), lambda i, j: (0, j)),
            side_in,
        ],
        out_specs=[pl.BlockSpec((tm, tn), lambda i, j: (i, j)), side_out],
        out_shape=[jax.ShapeDtypeStruct((T, N), _bf16), side_shape],
        compiler_params=_params(2),
        name="qkv_proj",
    )(xn, w16, colscale, w_o)


def _bf16_round(x):
    bits = struct.unpack("<I", struct.pack("<f", x))[0]
    bits = (bits + 0x7FFF + ((bits >> 16) & 1)) & 0xFFFF0000
    return struct.unpack("<f", struct.pack("<I", bits))[0]


def _bf16_split(x, n):
    terms = []
    for _ in range(n):
        t = _bf16_round(x)
        terms.append(t)
        x -= t
    return terms


LOG2E = math.log2(math.e)
LOG2E_TERMS = _bf16_split(LOG2E, 4)
KPOS_RADIX_BITS = 6


def _attn_kernel(slopes_ref, q_ref, k_ref, v_ref, lq1_ref, lk1_ref, lq2_ref, lk2_ref, sg_ref, o_ref,
                 kaug_ref, qaug_ref, s0_ref, s1_ref, mrun_ref, m_ref, lrun_ref, l_ref, acc_ref,
                 *, tq, lambda_init, unroll):
    d = DIFF_HEAD_DIM
    S = q_ref.shape[0]
    nq = S // tq
    nt = len(LOG2E_TERMS)
    slope = slopes_ref[pl.program_id(1)]
    s_refs = (s0_ref, s1_ref)

    pos = lax.broadcasted_iota(jnp.int32, (S, d), 0)
    lane = lax.broadcasted_iota(jnp.int32, (S, d), 1)
    lo = (pos & ((1 << KPOS_RADIX_BITS) - 1)).astype(_f32)
    hi = (pos >> KPOS_RADIX_BITS).astype(_f32) * float(1 << KPOS_RADIX_BITS)
    kaug_ref[...] = (slope * jnp.where(lane < nt, lo, jnp.where(lane < 2 * nt, hi, 0.0))).astype(_bf16)
    qlane = lax.broadcasted_iota(jnp.int32, (tq, d), 1)
    qaug = jnp.zeros((tq, d), _f32)
    for i, term in enumerate(LOG2E_TERMS):
        qaug = jnp.where((qlane == i) | (qlane == nt + i), term, qaug)
    qaug_ref[...] = qaug.astype(_bf16)

    lam = (jnp.exp(jnp.sum(lq1_ref[...] * lk1_ref[...], keepdims=True))
           - jnp.exp(jnp.sum(lq2_ref[...] * lk2_ref[...], keepdims=True))
           + lambda_init)

    def rows(j):
        if isinstance(j, int):
            return pl.ds(j * tq, tq)
        return pl.ds(pl.multiple_of(j * tq, tq), tq)

    def lane_blocks(x):
        return [x[:, c0:c0 + LANES] for c0 in range(0, x.shape[1], LANES)]

    def scores_chunk(u, j):
        qi, comp = divmod(u, 2)
        cs = slice(comp * d, (comp + 1) * d)
        qa = jnp.concatenate([q_ref[rows(qi), cs], qaug_ref[...]], axis=1)
        ka = jnp.concatenate([k_ref[rows(j), cs], kaug_ref[rows(j), :]], axis=1)
        s = lax.dot_general(qa, ka, (((1,), (1,)), ((), ())), preferred_element_type=_f32)
        if isinstance(j, int) and j == qi:
            r = lax.broadcasted_iota(jnp.int32, (tq, tq), 0)
            c = lax.broadcasted_iota(jnp.int32, (tq, tq), 1)
            s = jnp.where(c <= r, s, -jnp.inf)
        s_refs[comp][j] = s
        mrun_ref[comp] = functools.reduce(jnp.maximum, lane_blocks(s), mrun_ref[comp])

    def probs_pv_chunk(u, j):
        comp = u % 2
        p_rows = []
        for r0 in range(0, tq, ROW_GROUP):
            rs = slice(r0, r0 + ROW_GROUP)
            m = m_ref[comp, rs, :]
            lsum = lrun_ref[comp, rs, :]
            p_blocks = []
            for sb in lane_blocks(s_refs[comp][j, rs, :]):
                pb = jnp.exp2(sb - m)
                lsum = lsum + pb
                p_blocks.append(pb.astype(_bf16))
            lrun_ref[comp, rs, :] = lsum
            p_rows.append(jnp.concatenate(p_blocks, axis=1))
        p = jnp.concatenate(p_rows, axis=0)
        acc_ref[comp] += jnp.dot(p, v_ref[rows(j), :], preferred_element_type=_f32)

    def start_scores(u):
        mrun_ref[u % 2] = jnp.full((tq, LANES), -jnp.inf, _f32)

    def finish_scores(u):
        comp = u % 2
        m_ref[comp] = jnp.broadcast_to(jnp.max(mrun_ref[comp], axis=-1, keepdims=True), (tq, LANES))
        lrun_ref[comp] = jnp.zeros((tq, LANES), _f32)
        acc_ref[comp] = jnp.zeros((tq, 2 * d), _f32)

    def finish_probs(u):
        comp = u % 2
        l_ref[comp] = jnp.broadcast_to(jnp.sum(lrun_ref[comp], axis=-1, keepdims=True), (tq, LANES))

    def finish_tile(qi):
        gain = sg_ref[...] * (1.0 - lambda_init)
        for r0 in range(0, tq, ROW_GROUP):
            rs = slice(r0, r0 + ROW_GROUP)
            inv = [1.0 / l_ref[comp, rs, :] for comp in range(2)]
            inv = [jnp.concatenate([x] * (2 * d // LANES), axis=1) for x in inv]
            o = acc_ref[0, rs, :] * inv[0] - lam * (acc_ref[1, rs, :] * inv[1])
            ms = jnp.mean(o * o, axis=-1, keepdims=True)
            on = o * lax.rsqrt(ms + RMS_EPS) * gain
            o_ref[pl.ds(qi * tq + r0, ROW_GROUP), :] = on.astype(o_ref.dtype)

    n_units = 2 * nq
    chunks = lambda u: u // 2 + 1 if 0 <= u < n_units else 0

    start_scores(0)
    for j in range(chunks(0)):
        scores_chunk(0, j)
    finish_scores(0)
    for u in range(n_units):
        n_s, n_p = chunks(u + 1), chunks(u)
        if n_s:
            start_scores(u + 1)
        n_common = min(n_p, n_s - 1) if n_s else n_p

        def chunk_step(j, u=u, n_s=n_s):
            if n_s:
                scores_chunk(u + 1, j)
            probs_pv_chunk(u, j)

        def body(j, carry, chunk_step=chunk_step):
            chunk_step(j)
            return carry

        if n_common:
            lax.fori_loop(0, n_common, body, 0, unroll=min(unroll, n_common))
        for j in range(n_common, max(n_s, n_p)):
            if j < n_s:
                scores_chunk(u + 1, j)
            if j < n_p:
                probs_pv_chunk(u, j)
        finish_probs(u)
        if u % 2 == 1:
            finish_tile(u // 2)
        if n_s:
            finish_scores(u + 1)


def _diff_attention(qkv, slopes, lq1, lk1, lq2, lk2, subln_g, *, B, S, H, lambda_init, attn_layer, tq=512, unroll=3):
    d, e = DIFF_HEAD_DIM, V_HEAD_DIM
    kern = functools.partial(_attn_kernel, tq=tq, lambda_init=lambda_init, unroll=unroll)
    return pl.pallas_call(
        kern,
        grid_spec=pltpu.PrefetchScalarGridSpec(
            num_scalar_prefetch=1,
            grid=(B, H),
            in_specs=[
                pl.BlockSpec((S, e), lambda b, h, sl: (b, h)),
                pl.BlockSpec((S, e), lambda b, h, sl: (b, H + h)),
                pl.BlockSpec((S, e), lambda b, h, sl: (b, 2 * H + h)),
                pl.BlockSpec((None, 1, d), lambda b, h, sl: (attn_layer, 0, 0)),
                pl.BlockSpec((None, 1, d), lambda b, h, sl: (attn_layer, 0, 0)),
                pl.BlockSpec((None, 1, d), lambda b, h, sl: (attn_layer, 0, 0)),
                pl.BlockSpec((None, 1, d), lambda b, h, sl: (attn_layer, 0, 0)),
                pl.BlockSpec((None, 1, e), lambda b, h, sl: (attn_layer, 0, 0)),
            ],
            out_specs=pl.BlockSpec((S, e), lambda b, h, sl: (b, h)),
            scratch_shapes=[
                pltpu.VMEM((S, d), _bf16),
                pltpu.VMEM((tq, d), _bf16),
                pltpu.VMEM((S // tq, tq, tq), _f32),
                pltpu.VMEM((S // tq, tq, tq), _f32),
                pltpu.VMEM((2, tq, LANES), _f32),
                pltpu.VMEM((2, tq, LANES), _f32),
                pltpu.VMEM((2, tq, LANES), _f32),
                pltpu.VMEM((2, tq, LANES), _f32),
                pltpu.VMEM((2, tq, e), _f32),
            ],
        ),
        out_shape=jax.ShapeDtypeStruct((B * S, H * e), _bf16),
        compiler_params=_params(2),
        name="diff_attention",
    )(slopes, qkv, qkv, qkv, _rows(lq1), _rows(lk1), _rows(lq2), _rows(lk2), _rows(subln_g))


def _wo_kernel(o_ref, w_ref, res_ref, g_ref, h_ref, xn_ref):
    h = res_ref[...] + jnp.dot(o_ref[...], w_ref[...], preferred_element_type=_f32)
    h_ref[...] = h
    xn_ref[...] = _rms(h, g_ref[...]).astype(_bf16)


def _wo_proj(o, w16, res, norm_ffn, *, layer, tm=512):
    T, D = res.shape
    return pl.pallas_call(
        _wo_kernel,
        grid=(T // tm,),
        in_specs=[
            pl.BlockSpec((tm, D), lambda i: (i, 0)),
            pl.BlockSpec((D, D), lambda i: (0, 0), pipeline_mode=pl.Buffered(1)),
            pl.BlockSpec((tm, D), lambda i: (i, 0)),
            pl.BlockSpec((None, 1, D), lambda i: (layer, 0, 0)),
        ],
        out_specs=[
            pl.BlockSpec((tm, D), lambda i: (i, 0)),
            pl.BlockSpec((tm, D), lambda i: (i, 0)),
        ],
        out_shape=[
            jax.ShapeDtypeStruct((T, D), _f32),
            jax.ShapeDtypeStruct((T, D), _bf16),
        ],
        compiler_params=_params(1),
        name="wo_proj",
    )(o, w16, res, _rows(norm_ffn))


def kernel(x, norm_mix, norm_ffn, pool_w, pool_scale, w_qkv, lambda_q1, lambda_k1, lambda_q2, lambda_k2,
           subln_g, w_o, w_gate, w_up, w_down, final_norm):
    B, S, D = x.shape
    T = B * S
    H = D // V_HEAD_DIM

    h, xn = _pool_layer(x, norm_mix, pool_w, pool_scale, norm_ffn, layer=0, pool_layer=0)
    hmid, wd16, wqkv16 = _ffn_up(xn.reshape(T, D), w_gate, w_up, [(w_down, 0), (w_qkv, 0)], layer=0)
    h, xn = _ffn_down(hmid, wd16, h.reshape(T, D), norm_mix, 1, emit_residual=True)

    lambda_init = 0.8 - 0.6 * math.exp(-0.3 * 1)
    colscale = jnp.concatenate(
        [jnp.full((1, D), DIFF_HEAD_DIM ** -0.5 * LOG2E, _f32), jnp.ones((1, 2 * D), _f32)], axis=1)
    qkv, wo16 = _qkv_proj(xn, wqkv16, colscale, w_o, attn_layer=0)
    slopes = jnp.asarray([2.0 ** (-8.0 * (i + 1) / H) for i in range(H)], dtype=_f32)
    o = _diff_attention(qkv, slopes, lambda_q1, lambda_k1, lambda_q2, lambda_k2, subln_g,
                        B=B, S=S, H=H, lambda_init=lambda_init, attn_layer=0)
    h, xn = _wo_proj(o, wo16, h, norm_ffn, layer=1)
    hmid, wd16 = _ffn_up(xn, w_gate, w_up, [(w_down, 1)], layer=1)
    (out,) = _ffn_down(hmid, wd16, h, final_norm.reshape(1, D), 0, emit_residual=False)
    return out.reshape(B, S, D)
```

```python
import functools
import math
import struct

import jax
import jax.numpy as jnp
from jax import lax
from jax.experimental import pallas as pl
from jax.experimental.pallas import tpu as pltpu

RMS_EPS = 1e-6
POOL_WINDOWS = (2, 4, 8, 16)
POOL_HALO = 16
DIFF_HEAD_DIM = 128
V_HEAD_DIM = 2 * DIFF_HEAD_DIM
LANES = 128
ROW_GROUP = 64
FFN_UP_ROWS = 256
VMEM_LIMIT_BYTES = 127 * 512 * 1024

_f32 = jnp.float32
_bf16 = jnp.bfloat16


def _params(n_axes, flags=None):
    return pltpu.CompilerParams(
        dimension_semantics=("arbitrary",) * n_axes,
        vmem_limit_bytes=VMEM_LIMIT_BYTES,
        flags=flags,
    )


def _rows(a):
    return a.reshape(a.shape[0], 1, a.shape[1])


def _rms(x, g):
    ms = jnp.mean(x * x, axis=-1, keepdims=True)
    return x * lax.rsqrt(ms + RMS_EPS) * g


def _pool_kernel(x_ref, halo_ref, gmix_ref, w_ref, scale_ref, gffn_ref, h_ref, xn_ref, *, ts, group):
    s = pl.program_id(1)
    x = x_ref[0]
    g = gmix_ref[...]
    hn = _rms(x, g)
    halo = jnp.where(s > 0, _rms(halo_ref[0], g), 0.0)
    ext = jnp.concatenate([halo, hn], axis=0)
    t = s * ts + lax.broadcasted_iota(jnp.int32, (ts, 1), 0)
    ssq = jnp.zeros((ts, 1), _f32)
    for gi, w in enumerate(POOL_WINDOWS):
        cols = slice(gi * group, (gi + 1) * group)
        acc = ext[:, cols]
        span = 1
        while span < w:
            acc = acc + pltpu.roll(acc, span, axis=0)
            span *= 2
        cnt = jnp.minimum(t + 1, w).astype(_f32)
        pooled = acc[POOL_HALO:, :] / cnt
        diff = (pooled - hn[:, cols]).astype(_bf16)
        y = jnp.dot(diff, w_ref[gi].astype(_bf16), preferred_element_type=_f32)
        hg = x[:, cols] + y * scale_ref[:, cols]
        h_ref[0, :, cols] = hg
        ssq = ssq + jnp.sum(hg * hg, axis=-1, keepdims=True)
    d_model = x.shape[-1]
    inv = lax.rsqrt(ssq / d_model + RMS_EPS)
    xn_ref[0] = (h_ref[0] * inv * gffn_ref[...]).astype(_bf16)


def _pool_layer(x, norm_mix, pool_w, pool_scale, norm_ffn, *, layer, pool_layer, ts=512):
    B, S, D = x.shape
    _, G, C, _ = pool_w.shape
    halo_blocks = ts // POOL_HALO
    kern = functools.partial(_pool_kernel, ts=ts, group=C)
    return pl.pallas_call(
        kern,
        grid=(B, S // ts),
        in_specs=[
            pl.BlockSpec((1, ts, D), lambda b, s: (b, s, 0)),
            pl.BlockSpec((1, POOL_HALO, D), lambda b, s: (b, jnp.maximum(s * halo_blocks - 1, 0), 0)),
            pl.BlockSpec((None, 1, D), lambda b, s: (layer, 0, 0)),
            pl.BlockSpec((None, G, C, C), lambda b, s: (pool_layer, 0, 0, 0)),
            pl.BlockSpec((None, 1, D), lambda b, s: (pool_layer, 0, 0)),
            pl.BlockSpec((None, 1, D), lambda b, s: (layer, 0, 0)),
        ],
        out_specs=[
            pl.BlockSpec((1, ts, D), lambda b, s: (b, s, 0)),
            pl.BlockSpec((1, ts, D), lambda b, s: (b, s, 0)),
        ],
        out_shape=[
            jax.ShapeDtypeStruct((B, S, D), _f32),
            jax.ShapeDtypeStruct((B, S, D), _bf16),
        ],
        compiler_params=_params(2),
        name="pool_mixer",
    )(x, x, _rows(norm_mix), pool_w, _rows(pool_scale), _rows(norm_ffn))


def _ffn_up_kernel(xn_ref, wg_ref, wu_ref, *refs, n_side):
    side_in, o_ref, side_out = refs[:n_side], refs[n_side], refs[n_side + 1:]
    wg = wg_ref[...].astype(_bf16)
    wu = wu_ref[...].astype(_bf16)
    for r0 in range(0, xn_ref.shape[0], FFN_UP_ROWS):
        rs = slice(r0, r0 + FFN_UP_ROWS)
        xn = xn_ref[rs, :]
        gate = jnp.dot(xn, wg, preferred_element_type=_f32)
        up = jnp.dot(xn, wu, preferred_element_type=_f32)
        o_ref[rs, :] = (gate / (1.0 + jnp.exp(-gate)) * up).astype(o_ref.dtype)
    for src, dst in zip(side_in, side_out):
        dst[...] = src[...].astype(_bf16)


def _side_cast_specs(w, layer, n_steps, step_of):
    _, R, C = w.shape
    bf16_rows = 16
    slab = next(s for s in range(bf16_rows, R + 1, bf16_rows) if R % s == 0 and R // s <= n_steps)
    last = R // slab - 1
    return (pl.BlockSpec((None, slab, C), lambda *g: (layer, jnp.minimum(step_of(*g), last), 0)),
            pl.BlockSpec((slab, C), lambda *g: (jnp.minimum(step_of(*g), last), 0)),
            jax.ShapeDtypeStruct((R, C), _bf16))


def _ffn_up(xn, w_gate, w_up, side_weights, *, layer, tm=2048, tn=512):
    T, D = xn.shape
    F = w_gate.shape[2]
    ni, nj = T // tm, F // tn
    side = [_side_cast_specs(w, l, ni * nj, lambda i, j: i * nj + j) for w, l in side_weights]
    return pl.pallas_call(
        functools.partial(_ffn_up_kernel, n_side=len(side)),
        grid=(ni, nj),
        in_specs=[
            pl.BlockSpec((tm, D), lambda i, j: (i, 0)),
            pl.BlockSpec((None, D, tn), lambda i, j: (layer, 0, j)),
            pl.BlockSpec((None, D, tn), lambda i, j: (layer, 0, j)),
        ] + [s[0] for s in side],
        out_specs=[pl.BlockSpec((tm, tn), lambda i, j: (i, j))] + [s[1] for s in side],
        out_shape=[jax.ShapeDtypeStruct((T, F), _bf16)] + [s[2] for s in side],
        compiler_params=_params(2),
        name="ffn_up",
    )(xn, w_gate, w_up, *[w for w, _ in side_weights])


def _ffn_down_kernel(hm_ref, wd_ref, res_ref, g_ref, *outs, emit_residual):
    h = res_ref[...] + jnp.dot(hm_ref[...], wd_ref[...], preferred_element_type=_f32)
    if emit_residual:
        outs[0][...] = h
        outs[1][...] = _rms(h, g_ref[...]).astype(_bf16)
    else:
        outs[0][...] = _rms(h, g_ref[...])


def _ffn_down(hmid, wd16, res, g, g_row, *, emit_residual, tm=512):
    T, F = hmid.shape
    D = wd16.shape[1]
    row_spec = pl.BlockSpec((tm, D), lambda i: (i, 0))
    if emit_residual:
        out_specs = [row_spec, row_spec]
        out_shape = [jax.ShapeDtypeStruct((T, D), _f32), jax.ShapeDtypeStruct((T, D), _bf16)]
    else:
        out_specs = [row_spec]
        out_shape = [jax.ShapeDtypeStruct((T, D), _f32)]
    return pl.pallas_call(
        functools.partial(_ffn_down_kernel, emit_residual=emit_residual),
        grid=(T // tm,),
        in_specs=[
            pl.BlockSpec((tm, F), lambda i: (i, 0)),
            pl.BlockSpec((F, D), lambda i: (0, 0), pipeline_mode=pl.Buffered(1)),
            row_spec,
            pl.BlockSpec((None, 1, D), lambda i: (g_row, 0, 0)),
        ],
        out_specs=out_specs,
        out_shape=out_shape,
        compiler_params=_params(1),
        name="ffn_down",
    )(hmid, wd16, res, _rows(g))


def _qkv_kernel(xn_ref, w_ref, cs_ref, wo_ref, o_ref, wo16_ref):
    acc = jnp.dot(xn_ref[...], w_ref[...], preferred_element_type=_f32)
    o_ref[...] = (acc * cs_ref[...]).astype(o_ref.dtype)
    wo16_ref[...] = wo_ref[...].astype(_bf16)


def _qkv_proj(xn, w16, colscale, w_o, *, attn_layer, tm=1024, tn=1536):
    T, D = xn.shape
    N = w16.shape[1]
    ni, nj = T // tm, N // tn
    side_in, side_out, side_shape = _side_cast_specs(w_o, attn_layer, ni * nj, lambda i, j: i * nj + j)
    return pl.pallas_call(
        _qkv_kernel,
        grid=(ni, nj),
        in_specs=[
            pl.BlockSpec((tm, D), lambda i, j: (i, 0)),
            pl.BlockSpec((D, tn), lambda i, j: (0, j)),
            pl.BlockSpec((1, tn), lambda i, j: (0, j)),
            side_in,
        ],
        out_specs=[pl.BlockSpec((tm, tn), lambda i, j: (i, j)), side_out],
        out_shape=[jax.ShapeDtypeStruct((T, N), _bf16), side_shape],
        compiler_params=_params(2),
        name="qkv_proj",
    )(xn, w16, colscale, w_o)


def _bf16_round(x):
    bits = struct.unpack("<I", struct.pack("<f", x))[0]
    bits = (bits + 0x7FFF + ((bits >> 16) & 1)) & 0xFFFF0000
    return struct.unpack("<f", struct.pack("<I", bits))[0]


def _bf16_split(x, n):
    terms = []
    for _ in range(n):
        t = _bf16_round(x)
        terms.append(t)
        x -= t
    return terms


LOG2E = math.log2(math.e)
LOG2E_TERMS = _bf16_split(LOG2E, 4)
KPOS_RADIX_BITS = 6


def _attn_kernel(slopes_ref, q_ref, k_ref, v_ref, lq1_ref, lk1_ref, lq2_ref, lk2_ref, sg_ref, o_ref,
                 kaug_ref, qaug_ref, s0_ref, s1_ref, mrun_ref, m_ref, lrun_ref, l_ref, acc_ref,
                 *, tq, lambda_init, unroll):
    d = DIFF_HEAD_DIM
    S = q_ref.shape[0]
    nq = S // tq
    nt = len(LOG2E_TERMS)
    slope = slopes_ref[pl.program_id(1)]
    s_refs = (s0_ref, s1_ref)

    pos = lax.broadcasted_iota(jnp.int32, (S, d), 0)
    lane = lax.broadcasted_iota(jnp.int32, (S, d), 1)
    lo = (pos & ((1 << KPOS_RADIX_BITS) - 1)).astype(_f32)
    hi = (pos >> KPOS_RADIX_BITS).astype(_f32) * float(1 << KPOS_RADIX_BITS)
    kaug_ref[...] = (slope * jnp.where(lane < nt, lo, jnp.where(lane < 2 * nt, hi, 0.0))).astype(_bf16)
    qlane = lax.broadcasted_iota(jnp.int32, (tq, d), 1)
    qaug = jnp.zeros((tq, d), _f32)
    for i, term in enumerate(LOG2E_TERMS):
        qaug = jnp.where((qlane == i) | (qlane == nt + i), term, qaug)
    qaug_ref[...] = qaug.astype(_bf16)

    lam = (jnp.exp(jnp.sum(lq1_ref[...] * lk1_ref[...], keepdims=True))
           - jnp.exp(jnp.sum(lq2_ref[...] * lk2_ref[...], keepdims=True))
           + lambda_init)

    def rows(j):
        if isinstance(j, int):
            return pl.ds(j * tq, tq)
        return pl.ds(pl.multiple_of(j * tq, tq), tq)

    def lane_blocks(x):
        return [x[:, c0:c0 + LANES] for c0 in range(0, x.shape[1], LANES)]

    def scores_chunk(u, j):
        qi, comp = divmod(u, 2)
        cs = slice(comp * d, (comp + 1) * d)
        qa = jnp.concatenate([q_ref[rows(qi), cs], qaug_ref[...]], axis=1)
        ka = jnp.concatenate([k_ref[rows(j), cs], kaug_ref[rows(j), :]], axis=1)
        s = lax.dot_general(qa, ka, (((1,), (1,)), ((), ())), preferred_element_type=_f32)
        if isinstance(j, int) and j == qi:
            r = lax.broadcasted_iota(jnp.int32, (tq, tq), 0)
            c = lax.broadcasted_iota(jnp.int32, (tq, tq), 1)
            s = jnp.where(c <= r, s, -jnp.inf)
        s_refs[comp][j] = s
        mrun_ref[comp] = functools.reduce(jnp.maximum, lane_blocks(s), mrun_ref[comp])

    def probs_pv_chunk(u, j):
        comp = u % 2
        p_rows = []
        for r0 in range(0, tq, ROW_GROUP):
            rs = slice(r0, r0 + ROW_GROUP)
            m = m_ref[comp, rs, :]
            lsum = lrun_ref[comp, rs, :]
            p_blocks = []
            for sb in lane_blocks(s_refs[comp][j, rs, :]):
                pb = jnp.exp2(sb - m)
                lsum = lsum + pb
                p_blocks.append(pb.astype(_bf16))
            lrun_ref[comp, rs, :] = lsum
            p_rows.append(jnp.concatenate(p_blocks, axis=1))
        p = jnp.concatenate(p_rows, axis=0)
        acc_ref[comp] += jnp.dot(p, v_ref[rows(j), :], preferred_element_type=_f32)

    def start_scores(u):
        mrun_ref[u % 2] = jnp.full((tq, LANES), -jnp.inf, _f32)

    def finish_scores(u):
        comp = u % 2
        m_ref[comp] = jnp.broadcast_to(jnp.max(mrun_ref[comp], axis=-1, keepdims=True), (tq, LANES))
        lrun_ref[comp] = jnp.zeros((tq, LANES), _f32)
        acc_ref[comp] = jnp.zeros((tq, 2 * d), _f32)

    def finish_probs(u):
        comp = u % 2
        l_ref[comp] = jnp.broadcast_to(jnp.sum(lrun_ref[comp], axis=-1, keepdims=True), (tq, LANES))

    def finish_tile(qi):
        gain = sg_ref[...] * (1.0 - lambda_init)
        for r0 in range(0, tq, ROW_GROUP):
            rs = slice(r0, r0 + ROW_GROUP)
            inv = [1.0 / l_ref[comp, rs, :] for comp in range(2)]
            inv = [jnp.concatenate([x] * (2 * d // LANES), axis=1) for x in inv]
            o = acc_ref[0, rs, :] * inv[0] - lam * (acc_ref[1, rs, :] * inv[1])
            ms = jnp.mean(o * o, axis=-1, keepdims=True)
            on = o * lax.rsqrt(ms + RMS_EPS) * gain
            o_ref[pl.ds(qi * tq + r0, ROW_GROUP), :] = on.astype(o_ref.dtype)

    n_units = 2 * nq
    chunks = lambda u: u // 2 + 1 if 0 <= u < n_units else 0

    start_scores(0)
    for j in range(chunks(0)):
        scores_chunk(0, j)
    finish_scores(0)
    for u in range(n_units):
        n_s, n_p = chunks(u + 1), chunks(u)
        if n_s:
            start_scores(u + 1)
        n_common = min(n_p, n_s - 1) if n_s else n_p

        def chunk_step(j, u=u, n_s=n_s):
            if n_s:
                scores_chunk(u + 1, j)
            probs_pv_chunk(u, j)

        def body(j, carry, chunk_step=chunk_step):
            chunk_step(j)
            return carry

        if n_common:
            lax.fori_loop(0, n_common, body, 0, unroll=min(unroll, n_common))
        for j in range(n_common, max(n_s, n_p)):
            if j < n_s:
                scores_chunk(u + 1, j)
            if j < n_p:
                probs_pv_chunk(u, j)
        finish_probs(u)
        if u % 2 == 1:
            finish_tile(u // 2)
        if n_s:
            finish_scores(u + 1)


def _diff_attention(qkv, slopes, lq1, lk1, lq2, lk2, subln_g, *, B, S, H, lambda_init, attn_layer, tq=512, unroll=3):
    d, e = DIFF_HEAD_DIM, V_HEAD_DIM
    kern = functools.partial(_attn_kernel, tq=tq, lambda_init=lambda_init, unroll=unroll)
    return pl.pallas_call(
        kern,
        grid_spec=pltpu.PrefetchScalarGridSpec(
            num_scalar_prefetch=1,
            grid=(B, H),
            in_specs=[
                pl.BlockSpec((S, e), lambda b, h, sl: (b, h)),
                pl.BlockSpec((S, e), lambda b, h, sl: (b, H + h)),
                pl.BlockSpec((S, e), lambda b, h, sl: (b, 2 * H + h)),
                pl.BlockSpec((None, 1, d), lambda b, h, sl: (attn_layer, 0, 0)),
                pl.BlockSpec((None, 1, d), lambda b, h, sl: (attn_layer, 0, 0)),
                pl.BlockSpec((None, 1, d), lambda b, h, sl: (attn_layer, 0, 0)),
                pl.BlockSpec((None, 1, d), lambda b, h, sl: (attn_layer, 0, 0)),
                pl.BlockSpec((None, 1, e), lambda b, h, sl: (attn_layer, 0, 0)),
            ],
            out_specs=pl.BlockSpec((S, e), lambda b, h, sl: (b, h)),
            scratch_shapes=[
                pltpu.VMEM((S, d), _bf16),
                pltpu.VMEM((tq, d), _bf16),
                pltpu.VMEM((S // tq, tq, tq), _f32),
                pltpu.VMEM((S // tq, tq, tq), _f32),
                pltpu.VMEM((2, tq, LANES), _f32),
                pltpu.VMEM((2, tq, LANES), _f32),
                pltpu.VMEM((2, tq, LANES), _f32),
                pltpu.VMEM((2, tq, LANES), _f32),
                pltpu.VMEM((2, tq, e), _f32),
            ],
        ),
        out_shape=jax.ShapeDtypeStruct((B * S, H * e), _bf16),
        compiler_params=_params(2),
        name="diff_attention",
    )(slopes, qkv, qkv, qkv, _rows(lq1), _rows(lk1), _rows(lq2), _rows(lk2), _rows(subln_g))


def _wo_kernel(o_ref, w_ref, res_ref, g_ref, h_ref, xn_ref):
    h = res_ref[...] + jnp.dot(o_ref[...], w_ref[...], preferred_element_type=_f32)
    h_ref[...] = h
    xn_ref[...] = _rms(h, g_ref[...]).astype(_bf16)


def _wo_proj(o, w16, res, norm_ffn, *, layer, tm=512):
    T, D = res.shape
    return pl.pallas_call(
        _wo_kernel,
        grid=(T // tm,),
        in_specs=[
            pl.BlockSpec((tm, D), lambda i: (i, 0)),
            pl.BlockSpec((D, D), lambda i: (0, 0), pipeline_mode=pl.Buffered(1)),
            pl.BlockSpec((tm, D), lambda i: (i, 0)),
            pl.BlockSpec((None, 1, D), lambda i: (layer, 0, 0)),
        ],
        out_specs=[
            pl.BlockSpec((tm, D), lambda i: (i, 0)),
            pl.BlockSpec((tm, D), lambda i: (i, 0)),
        ],
        out_shape=[
            jax.ShapeDtypeStruct((T, D), _f32),
            jax.ShapeDtypeStruct((T, D), _bf16),
        ],
        compiler_params=_params(1),
        name="wo_proj",
    )(o, w16, res, _rows(norm_ffn))


def kernel(x, norm_mix, norm_ffn, pool_w, pool_scale, w_qkv, lambda_q1, lambda_k1, lambda_q2, lambda_k2,
           subln_g, w_o, w_gate, w_up, w_down, final_norm):
    B, S, D = x.shape
    T = B * S
    H = D // V_HEAD_DIM

    h, xn = _pool_layer(x, norm_mix, pool_w, pool_scale, norm_ffn, layer=0, pool_layer=0)
    hmid, wd16, wqkv16 = _ffn_up(xn.reshape(T, D), w_gate, w_up, [(w_down, 0), (w_qkv, 0)], layer=0)
    h, xn = _ffn_down(hmid, wd16, h.reshape(T, D), norm_mix, 1, emit_residual=True)

    lambda_init = 0.8 - 0.6 * math.exp(-0.3 * 1)
    colscale = jnp.concatenate(
        [jnp.full((1, D), DIFF_HEAD_DIM ** -0.5 * LOG2E, _f32), jnp.ones((1, 2 * D), _f32)], axis=1)
    qkv, wo16 = _qkv_proj(xn, wqkv16, colscale, w_o, attn_layer=0)
    slopes = jnp.asarray([2.0 ** (-8.0 * (i + 1) / H) for i in range(H)], dtype=_f32)
    o = _diff_attention(qkv, slopes, lambda_q1, lambda_k1, lambda_q2, lambda_k2, subln_g,
                        B=B, S=S, H=H, lambda_init=lambda_init, attn_layer=0)
    h, xn = _wo_proj(o, wo16, h, norm_ffn, layer=1)
    hmid, wd16 = _ffn_up(xn, w_gate, w_up, [(w_down, 1)], layer=1)
    (out,) = _ffn_down(hmid, wd16, h, final_norm.reshape(1, D), 0, emit_residual=False)
    return out.reshape(B, S, D)
```

```python
import functools
import math
import struct

import jax
import jax.numpy as jnp
from jax import lax
from jax.experimental import pallas as pl
from jax.experimental.pallas import tpu as pltpu

RMS_EPS = 1e-6
POOL_WINDOWS = (2, 4, 8, 16)
POOL_HALO = 16
DIFF_HEAD_DIM = 128
V_HEAD_DIM = 2 * DIFF_HEAD_DIM
LANES = 128
ROW_GROUP = 64
FFN_UP_ROWS = 128
VMEM_LIMIT_BYTES = 127 * 512 * 1024

_f32 = jnp.float32
_bf16 = jnp.bfloat16


def _params(n_axes, flags=None):
    return pltpu.CompilerParams(
        dimension_semantics=("arbitrary",) * n_axes,
        vmem_limit_bytes=VMEM_LIMIT_BYTES,
        flags=flags,
    )


def _rows(a):
    return a.reshape(a.shape[0], 1, a.shape[1])


def _rms(x, g):
    ms = jnp.mean(x * x, axis=-1, keepdims=True)
    return x * lax.rsqrt(ms + RMS_EPS) * g


def _pool_kernel(x_ref, halo_ref, gmix_ref, w_ref, scale_ref, gffn_ref, h_ref, xn_ref, *, ts, group):
    s = pl.program_id(1)
    x = x_ref[0]
    g = gmix_ref[...]
    hn = _rms(x, g)
    halo = jnp.where(s > 0, _rms(halo_ref[0], g), 0.0)
    ext = jnp.concatenate([halo, hn], axis=0)
    t = s * ts + lax.broadcasted_iota(jnp.int32, (ts, 1), 0)
    ssq = jnp.zeros((ts, 1), _f32)
    for gi, w in enumerate(POOL_WINDOWS):
        cols = slice(gi * group, (gi + 1) * group)
        acc = ext[:, cols]
        span = 1
        while span < w:
            acc = acc + pltpu.roll(acc, span, axis=0)
            span *= 2
        cnt = jnp.minimum(t + 1, w).astype(_f32)
        pooled = acc[POOL_HALO:, :] / cnt
        diff = (pooled - hn[:, cols]).astype(_bf16)
        y = jnp.dot(diff, w_ref[gi].astype(_bf16), preferred_element_type=_f32)
        hg = x[:, cols] + y * scale_ref[:, cols]
        h_ref[0, :, cols] = hg
        ssq = ssq + jnp.sum(hg * hg, axis=-1, keepdims=True)
    d_model = x.shape[-1]
    inv = lax.rsqrt(ssq / d_model + RMS_EPS)
    xn_ref[0] = (h_ref[0] * inv * gffn_ref[...]).astype(_bf16)


def _pool_layer(x, norm_mix, pool_w, pool_scale, norm_ffn, *, layer, pool_layer, ts=512):
    B, S, D = x.shape
    _, G, C, _ = pool_w.shape
    halo_blocks = ts // POOL_HALO
    kern = functools.partial(_pool_kernel, ts=ts, group=C)
    return pl.pallas_call(
        kern,
        grid=(B, S // ts),
        in_specs=[
            pl.BlockSpec((1, ts, D), lambda b, s: (b, s, 0)),
            pl.BlockSpec((1, POOL_HALO, D), lambda b, s: (b, jnp.maximum(s * halo_blocks - 1, 0), 0)),
            pl.BlockSpec((None, 1, D), lambda b, s: (layer, 0, 0)),
            pl.BlockSpec((None, G, C, C), lambda b, s: (pool_layer, 0, 0, 0)),
            pl.BlockSpec((None, 1, D), lambda b, s: (pool_layer, 0, 0)),
            pl.BlockSpec((None, 1, D), lambda b, s: (layer, 0, 0)),
        ],
        out_specs=[
            pl.BlockSpec((1, ts, D), lambda b, s: (b, s, 0)),
            pl.BlockSpec((1, ts, D), lambda b, s: (b, s, 0)),
        ],
        out_shape=[
            jax.ShapeDtypeStruct((B, S, D), _f32),
            jax.ShapeDtypeStruct((B, S, D), _bf16),
        ],
        compiler_params=_params(2),
        name="pool_mixer",
    )(x, x, _rows(norm_mix), pool_w, _rows(pool_scale), _rows(norm_ffn))


def _ffn_up_kernel(xn_ref, wg_ref, wu_ref, *refs, n_side):
    side_in, o_ref, side_out = refs[:n_side], refs[n_side], refs[n_side + 1:]
    wg = wg_ref[...].astype(_bf16)
    wu = wu_ref[...].astype(_bf16)
    for r0 in range(0, xn_ref.shape[0], FFN_UP_ROWS):
        rs = slice(r0, r0 + FFN_UP_ROWS)
        xn = xn_ref[rs, :]
        gate = jnp.dot(xn, wg, preferred_element_type=_f32)
        up = jnp.dot(xn, wu, preferred_element_type=_f32)
        o_ref[rs, :] = (gate / (1.0 + jnp.exp(-gate)) * up).astype(o_ref.dtype)
    for src, dst in zip(side_in, side_out):
        dst[...] = src[...].astype(_bf16)


def _side_cast_specs(w, layer, n_steps, step_of):
    _, R, C = w.shape
    bf16_rows = 16
    slab = next(s for s in range(bf16_rows, R + 1, bf16_rows) if R % s == 0 and R // s <= n_steps)
    last = R // slab - 1
    return (pl.BlockSpec((None, slab, C), lambda *g: (layer, jnp.minimum(step_of(*g), last), 0)),
            pl.BlockSpec((slab, C), lambda *g: (jnp.minimum(step_of(*g), last), 0)),
            jax.ShapeDtypeStruct((R, C), _bf16))


def _ffn_up(xn, w_gate, w_up, side_weights, *, layer, tm=2048, tn=512):
    T, D = xn.shape
    F = w_gate.shape[2]
    ni, nj = T // tm, F // tn
    side = [_side_cast_specs(w, l, ni * nj, lambda i, j: i * nj + j) for w, l in side_weights]
    return pl.pallas_call(
        functools.partial(_ffn_up_kernel, n_side=len(side)),
        grid=(ni, nj),
        in_specs=[
            pl.BlockSpec((tm, D), lambda i, j: (i, 0)),
            pl.BlockSpec((None, D, tn), lambda i, j: (layer, 0, j)),
            pl.BlockSpec((None, D, tn), lambda i, j: (layer, 0, j)),
        ] + [s[0] for s in side],
        out_specs=[pl.BlockSpec((tm, tn), lambda i, j: (i, j))] + [s[1] for s in side],
        out_shape=[jax.ShapeDtypeStruct((T, F), _bf16)] + [s[2] for s in side],
        compiler_params=_params(2),
        name="ffn_up",
    )(xn, w_gate, w_up, *[w for w, _ in side_weights])


def _ffn_down_kernel(hm_ref, wd_ref, res_ref, g_ref, *outs, emit_residual):
    h = res_ref[...] + jnp.dot(hm_ref[...], wd_ref[...], preferred_element_type=_f32)
    if emit_residual:
        outs[0][...] = h
        outs[1][...] = _rms(h, g_ref[...]).astype(_bf16)
    else:
        outs[0][...] = _rms(h, g_ref[...])


def _ffn_down(hmid, wd16, res, g, g_row, *, emit_residual, tm=512):
    T, F = hmid.shape
    D = wd16.shape[1]
    row_spec = pl.BlockSpec((tm, D), lambda i: (i, 0))
    if emit_residual:
        out_specs = [row_spec, row_spec]
        out_shape = [jax.ShapeDtypeStruct((T, D), _f32), jax.ShapeDtypeStruct((T, D), _bf16)]
    else:
        out_specs = [row_spec]
        out_shape = [jax.ShapeDtypeStruct((T, D), _f32)]
    return pl.pallas_call(
        functools.partial(_ffn_down_kernel, emit_residual=emit_residual),
        grid=(T // tm,),
        in_specs=[
            pl.BlockSpec((tm, F), lambda i: (i, 0)),
            pl.BlockSpec((F, D), lambda i: (0, 0), pipeline_mode=pl.Buffered(1)),
            row_spec,
            pl.BlockSpec((None, 1, D), lambda i: (g_row, 0, 0)),
        ],
        out_specs=out_specs,
        out_shape=out_shape,
        compiler_params=_params(1),
        name="ffn_down",
    )(hmid, wd16, res, _rows(g))


def _qkv_kernel(xn_ref, w_ref, cs_ref, wo_ref, o_ref, wo16_ref):
    acc = jnp.dot(xn_ref[...], w_ref[...], preferred_element_type=_f32)
    o_ref[...] = (acc * cs_ref[...]).astype(o_ref.dtype)
    wo16_ref[...] = wo_ref[...].astype(_bf16)


def _qkv_proj(xn, w16, colscale, w_o, *, attn_layer, tm=2048, tn=1536):
    T, D = xn.shape
    N = w16.shape[1]
    ni, nj = T // tm, N // tn
    side_in, side_out, side_shape = _side_cast_specs(w_o, attn_layer, ni * nj, lambda i, j: i * nj + j)
    return pl.pallas_call(
        _qkv_kernel,
        grid=(ni, nj),
        in_specs=[
            pl.BlockSpec((tm, D), lambda i, j: (i, 0)),
            pl.BlockSpec((D, tn), lambda i, j: (0, j)),
            pl.BlockSpec((1, tn), lambda i, j: (0, j)),
            side_in,
        ],
        out_specs=[pl.BlockSpec((tm, tn), lambda i, j: (i, j)), side_out],
        out_shape=[jax.ShapeDtypeStruct((T, N), _bf16), side_shape],
        compiler_params=_params(2),
        name="qkv_proj",
    )(xn, w16, colscale, w_o)


def _bf16_round(x):
    bits = struct.unpack("<I", struct.pack("<f", x))[0]
    bits = (bits + 0x7FFF + ((bits >> 16) & 1)) & 0xFFFF0000
    return struct.unpack("<f", struct.pack("<I", bits))[0]


def _bf16_split(x, n):
    terms = []
    for _ in range(n):
        t = _bf16_round(x)
        terms.append(t)
        x -= t
    return terms


LOG2E = math.log2(math.e)
LOG2E_TERMS = _bf16_split(LOG2E, 4)
KPOS_RADIX_BITS = 6


def _attn_kernel(slopes_ref, q_ref, k_ref, v_ref, lq1_ref, lk1_ref, lq2_ref, lk2_ref, sg_ref, o_ref,
                 kaug_ref, qaug_ref, s0_ref, s1_ref, mrun_ref, m_ref, lrun_ref, l_ref, acc_ref,
                 *, tq, lambda_init, unroll):
    d = DIFF_HEAD_DIM
    S = q_ref.shape[0]
    nq = S // tq
    nt = len(LOG2E_TERMS)
    slope = slopes_ref[pl.program_id(1)]
    s_refs = (s0_ref, s1_ref)

    @pl.when((pl.program_id(0) == 0) & (pl.program_id(1) == 0))
    def _():
        pos = lax.broadcasted_iota(jnp.int32, (S, d), 0)
        lane = lax.broadcasted_iota(jnp.int32, (S, d), 1)
        lo = (pos & ((1 << KPOS_RADIX_BITS) - 1)).astype(_f32)
        hi = (pos >> KPOS_RADIX_BITS).astype(_f32) * float(1 << KPOS_RADIX_BITS)
        kaug_ref[...] = jnp.where(lane < nt, lo, jnp.where(lane < 2 * nt, hi, 0.0)).astype(_bf16)

    qlane = lax.broadcasted_iota(jnp.int32, (tq, d), 1)
    qaug = jnp.zeros((tq, d), _f32)
    for i, term in enumerate(LOG2E_TERMS):
        qaug = jnp.where((qlane == i) | (qlane == nt + i), slope * term, qaug)
    qaug_ref[...] = qaug.astype(_bf16)

    lam = (jnp.exp(jnp.sum(lq1_ref[...] * lk1_ref[...], keepdims=True))
           - jnp.exp(jnp.sum(lq2_ref[...] * lk2_ref[...], keepdims=True))
           + lambda_init)

    def rows(j):
        if isinstance(j, int):
            return pl.ds(j * tq, tq)
        return pl.ds(pl.multiple_of(j * tq, tq), tq)

    def lane_blocks(x):
        return [x[:, c0:c0 + LANES] for c0 in range(0, x.shape[1], LANES)]

    def scores_chunk(u, j):
        qi, comp = divmod(u, 2)
        cs = slice(comp * d, (comp + 1) * d)
        qa = jnp.concatenate([q_ref[rows(qi), cs], qaug_ref[...]], axis=1)
        ka = jnp.concatenate([k_ref[rows(j), cs], kaug_ref[rows(j), :]], axis=1)
        s = lax.dot_general(qa, ka, (((1,), (1,)), ((), ())), preferred_element_type=_f32)
        if isinstance(j, int) and j == qi:
            r = lax.broadcasted_iota(jnp.int32, (tq, tq), 0)
            c = lax.broadcasted_iota(jnp.int32, (tq, tq), 1)
            s = jnp.where(c <= r, s, -jnp.inf)
        s_refs[comp][j] = s
        mrun_ref[comp] = functools.reduce(jnp.maximum, lane_blocks(s), mrun_ref[comp])

    def probs_pv_chunk(u, j):
        qi, comp = divmod(u, 2)
        diagonal = isinstance(j, int) and j == qi
        p_rows = []
        for r0 in range(0, tq, ROW_GROUP):
            rs = slice(r0, r0 + ROW_GROUP)
            m = m_ref[comp, rs, :]
            lsum = lrun_ref[comp, rs, :]
            n_live = -(-(r0 + ROW_GROUP) // LANES) if diagonal else tq // LANES
            p_blocks = []
            for sb in lane_blocks(s_refs[comp][j, rs, 0:n_live * LANES]):
                pb = jnp.exp2(sb - m)
                lsum = lsum + pb
                p_blocks.append(pb.astype(_bf16))
            p_blocks += [jnp.zeros((ROW_GROUP, LANES), _bf16)] * (tq // LANES - n_live)
            lrun_ref[comp, rs, :] = lsum
            p_rows.append(jnp.concatenate(p_blocks, axis=1))
        p = jnp.concatenate(p_rows, axis=0)
        acc_ref[comp] += jnp.dot(p, v_ref[rows(j), :], preferred_element_type=_f32)

    def start_scores(u):
        mrun_ref[u % 2] = jnp.full((tq, LANES), -jnp.inf, _f32)

    def finish_scores(u):
        comp = u % 2
        m_ref[comp] = jnp.broadcast_to(jnp.max(mrun_ref[comp], axis=-1, keepdims=True), (tq, LANES))
        lrun_ref[comp] = jnp.zeros((tq, LANES), _f32)
        acc_ref[comp] = jnp.zeros((tq, 2 * d), _f32)

    def finish_probs(u):
        comp = u % 2
        l_ref[comp] = jnp.broadcast_to(jnp.sum(lrun_ref[comp], axis=-1, keepdims=True), (tq, LANES))

    def finish_tile(qi):
        gain = sg_ref[...] * (1.0 - lambda_init)
        for r0 in range(0, tq, ROW_GROUP):
            rs = slice(r0, r0 + ROW_GROUP)
            inv = [1.0 / l_ref[comp, rs, :] for comp in range(2)]
            inv = [jnp.concatenate([x] * (2 * d // LANES), axis=1) for x in inv]
            o = acc_ref[0, rs, :] * inv[0] - lam * (acc_ref[1, rs, :] * inv[1])
            ms = jnp.mean(o * o, axis=-1, keepdims=True)
            on = o * lax.rsqrt(ms + RMS_EPS) * gain
            o_ref[pl.ds(qi * tq + r0, ROW_GROUP), :] = on.astype(o_ref.dtype)

    n_units = 2 * nq
    chunks = lambda u: u // 2 + 1 if 0 <= u < n_units else 0

    start_scores(0)
    for j in range(chunks(0)):
        scores_chunk(0, j)
    finish_scores(0)
    for u in range(n_units):
        n_s, n_p = chunks(u + 1), chunks(u)
        if n_s:
            start_scores(u + 1)
        n_common = min(n_p - 1, n_s - 1) if n_s else n_p - 1

        def chunk_step(j, u=u, n_s=n_s):
            if n_s:
                scores_chunk(u + 1, j)
            probs_pv_chunk(u, j)

        def body(j, carry, chunk_step=chunk_step):
            chunk_step(j)
            return carry

        if n_common:
            lax.fori_loop(0, n_common, body, 0, unroll=min(unroll, n_common))
        for j in range(n_common, max(n_s, n_p)):
            if j < n_s:
                scores_chunk(u + 1, j)
            if j < n_p:
                probs_pv_chunk(u, j)
        finish_probs(u)
        if u % 2 == 1:
            finish_tile(u // 2)
        if n_s:
            finish_scores(u + 1)


def _diff_attention(qkv, slopes, lq1, lk1, lq2, lk2, subln_g, *, B, S, H, lambda_init, attn_layer, tq=512, unroll=3):
    d, e = DIFF_HEAD_DIM, V_HEAD_DIM
    kern = functools.partial(_attn_kernel, tq=tq, lambda_init=lambda_init, unroll=unroll)
    return pl.pallas_call(
        kern,
        grid_spec=pltpu.PrefetchScalarGridSpec(
            num_scalar_prefetch=1,
            grid=(B, H),
            in_specs=[
                pl.BlockSpec((S, e), lambda b, h, sl: (b, h)),
                pl.BlockSpec((S, e), lambda b, h, sl: (b, H + h)),
                pl.BlockSpec((S, e), lambda b, h, sl: (b, 2 * H + h)),
                pl.BlockSpec((None, 1, d), lambda b, h, sl: (attn_layer, 0, 0)),
                pl.BlockSpec((None, 1, d), lambda b, h, sl: (attn_layer, 0, 0)),
                pl.BlockSpec((None, 1, d), lambda b, h, sl: (attn_layer, 0, 0)),
                pl.BlockSpec((None, 1, d), lambda b, h, sl: (attn_layer, 0, 0)),
                pl.BlockSpec((None, 1, e), lambda b, h, sl: (attn_layer, 0, 0)),
            ],
            out_specs=pl.BlockSpec((S, e), lambda b, h, sl: (b, h)),
            scratch_shapes=[
                pltpu.VMEM((S, d), _bf16),
                pltpu.VMEM((tq, d), _bf16),
                pltpu.VMEM((S // tq, tq, tq), _f32),
                pltpu.VMEM((S // tq, tq, tq), _f32),
                pltpu.VMEM((2, tq, LANES), _f32),
                pltpu.VMEM((2, tq, LANES), _f32),
                pltpu.VMEM((2, tq, LANES), _f32),
                pltpu.VMEM((2, tq, LANES), _f32),
                pltpu.VMEM((2, tq, e), _f32),
            ],
        ),
        out_shape=jax.ShapeDtypeStruct((B * S, H * e), _bf16),
        compiler_params=_params(2),
        name="diff_attention",
    )(slopes, qkv, qkv, qkv, _rows(lq1), _rows(lk1), _rows(lq2), _rows(lk2), _rows(subln_g))


def _wo_kernel(o_ref, w_ref, res_ref, g_ref, h_ref, xn_ref):
    h = res_ref[...] + jnp.dot(o_ref[...], w_ref[...], preferred_element_type=_f32)
    h_ref[...] = h
    xn_ref[...] = _rms(h, g_ref[...]).astype(_bf16)


def _wo_proj(o, w16, res, norm_ffn, *, layer, tm=512):
    T, D = res.shape
    return pl.pallas_call(
        _wo_kernel,
        grid=(T // tm,),
        in_specs=[
            pl.BlockSpec((tm, D), lambda i: (i, 0)),
            pl.BlockSpec((D, D), lambda i: (0, 0), pipeline_mode=pl.Buffered(1)),
            pl.BlockSpec((tm, D), lambda i: (i, 0)),
            pl.BlockSpec((None, 1, D), lambda i: (layer, 0, 0)),
        ],
        out_specs=[
            pl.BlockSpec((tm, D), lambda i: (i, 0)),
            pl.BlockSpec((tm, D), lambda i: (i, 0)),
        ],
        out_shape=[
            jax.ShapeDtypeStruct((T, D), _f32),
            jax.ShapeDtypeStruct((T, D), _bf16),
        ],
        compiler_params=_params(1),
        name="wo_proj",
    )(o, w16, res, _rows(norm_ffn))


def kernel(x, norm_mix, norm_ffn, pool_w, pool_scale, w_qkv, lambda_q1, lambda_k1, lambda_q2, lambda_k2,
           subln_g, w_o, w_gate, w_up, w_down, final_norm):
    B, S, D = x.shape
    T = B * S
    H = D // V_HEAD_DIM

    h, xn = _pool_layer(x, norm_mix, pool_w, pool_scale, norm_ffn, layer=0, pool_layer=0)
    hmid, wd16, wqkv16 = _ffn_up(xn.reshape(T, D), w_gate, w_up, [(w_down, 0), (w_qkv, 0)], layer=0)
    h, xn = _ffn_down(hmid, wd16, h.reshape(T, D), norm_mix, 1, emit_residual=True)

    lambda_init = 0.8 - 0.6 * math.exp(-0.3 * 1)
    colscale = jnp.concatenate(
        [jnp.full((1, D), DIFF_HEAD_DIM ** -0.5 * LOG2E, _f32), jnp.ones((1, 2 * D), _f32)], axis=1)
    qkv, wo16 = _qkv_proj(xn, wqkv16, colscale, w_o, attn_layer=0)
    slope_values = [2.0 ** (-8.0 * (i + 1) / H) for i in range(H)]
    assert all(math.frexp(s)[0] == 0.5 for s in slope_values)
    slopes = jnp.asarray(slope_values, dtype=_f32)
    o = _diff_attention(qkv, slopes, lambda_q1, lambda_k1, lambda_q2, lambda_k2, subln_g,
                        B=B, S=S, H=H, lambda_init=lambda_init, attn_layer=0)
    h, xn = _wo_proj(o, wo16, h, norm_ffn, layer=1)
    hmid, wd16 = _ffn_up(xn, w_gate, w_up, [(w_down, 1)], layer=1)
    (out,) = _ffn_down(hmid, wd16, h, final_norm.reshape(1, D), 0, emit_residual=False)
    return out.reshape(B, S, D)
```

```python
import functools
import math
import struct

import jax
import jax.numpy as jnp
from jax import lax
from jax.experimental import pallas as pl
from jax.experimental.pallas import tpu as pltpu

RMS_EPS = 1e-6
POOL_WINDOWS = (2, 4, 8, 16)
POOL_HALO = 16
DIFF_HEAD_DIM = 128
V_HEAD_DIM = 2 * DIFF_HEAD_DIM
LANES = 128
ROW_GROUP = 64
FFN_UP_ROWS = 128
VMEM_LIMIT_BYTES = 127 * 512 * 1024

_f32 = jnp.float32
_bf16 = jnp.bfloat16


def _params(n_axes):
    return pltpu.CompilerParams(
        dimension_semantics=("arbitrary",) * n_axes,
        vmem_limit_bytes=VMEM_LIMIT_BYTES,
    )


def _rows(a):
    return a.reshape(a.shape[0], 1, a.shape[1])


def _rms(x, g):
    ms = jnp.mean(x * x, axis=-1, keepdims=True)
    return x * lax.rsqrt(ms + RMS_EPS) * g


def _pool_kernel(x_ref, halo_ref, gmix_ref, w_ref, scale_ref, gffn_ref, h_ref, xn_ref, *, ts, group):
    s = pl.program_id(1)
    x = x_ref[0]
    g = gmix_ref[...]
    hn = _rms(x, g)
    halo = jnp.where(s > 0, _rms(halo_ref[0], g), 0.0)
    ext = jnp.concatenate([halo, hn], axis=0)
    t = s * ts + lax.broadcasted_iota(jnp.int32, (ts, 1), 0)
    ssq = jnp.zeros((ts, 1), _f32)
    for gi, w in enumerate(POOL_WINDOWS):
        cols = slice(gi * group, (gi + 1) * group)
        acc = ext[:, cols]
        span = 1
        while span < w:
            acc = acc + pltpu.roll(acc, span, axis=0)
            span *= 2
        cnt = jnp.minimum(t + 1, w).astype(_f32)
        pooled = acc[POOL_HALO:, :] / cnt
        diff = (pooled - hn[:, cols]).astype(_bf16)
        y = jnp.dot(diff, w_ref[gi].astype(_bf16), preferred_element_type=_f32)
        hg = x[:, cols] + y * scale_ref[:, cols]
        h_ref[0, :, cols] = hg
        ssq = ssq + jnp.sum(hg * hg, axis=-1, keepdims=True)
    d_model = x.shape[-1]
    inv = lax.rsqrt(ssq / d_model + RMS_EPS)
    xn_ref[0] = (h_ref[0] * inv * gffn_ref[...]).astype(_bf16)


def _pool_layer(x, norm_mix, pool_w, pool_scale, norm_ffn, *, layer, pool_layer, ts=1024):
    B, S, D = x.shape
    _, G, C, _ = pool_w.shape
    halo_blocks = ts // POOL_HALO
    kern = functools.partial(_pool_kernel, ts=ts, group=C)
    return pl.pallas_call(
        kern,
        grid=(B, S // ts),
        in_specs=[
            pl.BlockSpec((1, ts, D), lambda b, s: (b, s, 0)),
            pl.BlockSpec((1, POOL_HALO, D), lambda b, s: (b, jnp.maximum(s * halo_blocks - 1, 0), 0)),
            pl.BlockSpec((None, 1, D), lambda b, s: (layer, 0, 0)),
            pl.BlockSpec((None, G, C, C), lambda b, s: (pool_layer, 0, 0, 0)),
            pl.BlockSpec((None, 1, D), lambda b, s: (pool_layer, 0, 0)),
            pl.BlockSpec((None, 1, D), lambda b, s: (layer, 0, 0)),
        ],
        out_specs=[
            pl.BlockSpec((1, ts, D), lambda b, s: (b, s, 0)),
            pl.BlockSpec((1, ts, D), lambda b, s: (b, s, 0)),
        ],
        out_shape=[
            jax.ShapeDtypeStruct((B, S, D), _f32),
            jax.ShapeDtypeStruct((B, S, D), _bf16),
        ],
        compiler_params=_params(2),
        name="pool_mixer",
    )(x, x, _rows(norm_mix), pool_w, _rows(pool_scale), _rows(norm_ffn))


def _ffn_up_kernel(xn_ref, wg_ref, wu_ref, *refs, n_side):
    side_in, o_ref, side_out = refs[:n_side], refs[n_side], refs[n_side + 1:]
    wg = wg_ref[...].astype(_bf16)
    wu = wu_ref[...].astype(_bf16)
    for r0 in range(0, xn_ref.shape[0], FFN_UP_ROWS):
        rs = slice(r0, r0 + FFN_UP_ROWS)
        xn = xn_ref[rs, :]
        gate = jnp.dot(xn, wg, preferred_element_type=_f32)
        up = jnp.dot(xn, wu, preferred_element_type=_f32)
        o_ref[rs, :] = (gate / (1.0 + jnp.exp(-gate)) * up).astype(o_ref.dtype)
    for src, dst in zip(side_in, side_out):
        dst[...] = src[...].astype(_bf16)


def _side_cast_specs(w, layer, n_steps, step_of):
    _, R, C = w.shape
    bf16_rows = 16
    slab = next(s for s in range(bf16_rows, R + 1, bf16_rows) if R % s == 0 and R // s <= n_steps)
    last = R // slab - 1
    return (pl.BlockSpec((None, slab, C), lambda *g: (layer, jnp.minimum(step_of(*g), last), 0)),
            pl.BlockSpec((slab, C), lambda *g: (jnp.minimum(step_of(*g), last), 0)),
            jax.ShapeDtypeStruct((R, C), _bf16))


def _ffn_up(xn, w_gate, w_up, side_weights, *, layer, tm=2048, tn=512):
    T, D = xn.shape
    F = w_gate.shape[2]
    ni, nj = T // tm, F // tn
    side = [_side_cast_specs(w, l, ni * nj, lambda i, j: i * nj + j) for w, l in side_weights]
    return pl.pallas_call(
        functools.partial(_ffn_up_kernel, n_side=len(side)),
        grid=(ni, nj),
        in_specs=[
            pl.BlockSpec((tm, D), lambda i, j: (i, 0)),
            pl.BlockSpec((None, D, tn), lambda i, j: (layer, 0, j)),
            pl.BlockSpec((None, D, tn), lambda i, j: (layer, 0, j)),
        ] + [s[0] for s in side],
        out_specs=[pl.BlockSpec((tm, tn), lambda i, j: (i, j))] + [s[1] for s in side],
        out_shape=[jax.ShapeDtypeStruct((T, F), _bf16)] + [s[2] for s in side],
        compiler_params=_params(2),
        name="ffn_up",
    )(xn, w_gate, w_up, *[w for w, _ in side_weights])


def _ffn_down_kernel(hm_ref, wd_ref, res_ref, g_ref, *outs, emit_residual):
    h = res_ref[...] + jnp.dot(hm_ref[...], wd_ref[...], preferred_element_type=_f32)
    if emit_residual:
        outs[0][...] = h
        outs[1][...] = _rms(h, g_ref[...]).astype(_bf16)
    else:
        outs[0][...] = _rms(h, g_ref[...])


def _ffn_down(hmid, wd16, res, g, g_row, *, emit_residual, tm=512):
    T, F = hmid.shape
    D = wd16.shape[1]
    row_spec = pl.BlockSpec((tm, D), lambda i: (i, 0))
    if emit_residual:
        out_specs = [row_spec, row_spec]
        out_shape = [jax.ShapeDtypeStruct((T, D), _f32), jax.ShapeDtypeStruct((T, D), _bf16)]
    else:
        out_specs = [row_spec]
        out_shape = [jax.ShapeDtypeStruct((T, D), _f32)]
    return pl.pallas_call(
        functools.partial(_ffn_down_kernel, emit_residual=emit_residual),
        grid=(T // tm,),
        in_specs=[
            pl.BlockSpec((tm, F), lambda i: (i, 0)),
            pl.BlockSpec((F, D), lambda i: (0, 0), pipeline_mode=pl.Buffered(1)),
            row_spec,
            pl.BlockSpec((None, 1, D), lambda i: (g_row, 0, 0)),
        ],
        out_specs=out_specs,
        out_shape=out_shape,
        compiler_params=_params(1),
        name="ffn_down",
    )(hmid, wd16, res, _rows(g))


def _qkv_kernel(xn_ref, w_ref, cs_ref, wo_ref, o_ref, wo16_ref):
    acc = jnp.dot(xn_ref[...], w_ref[...], preferred_element_type=_f32)
    o_ref[...] = (acc * cs_ref[...]).astype(o_ref.dtype)
    wo16_ref[...] = wo_ref[...].astype(_bf16)


def _qkv_proj(xn, w16, colscale, w_o, *, attn_layer, tm=2048, tn=1536):
    T, D = xn.shape
    N = w16.shape[1]
    ni, nj = T // tm, N // tn
    side_in, side_out, side_shape = _side_cast_specs(w_o, attn_layer, ni * nj, lambda i, j: i * nj + j)
    return pl.pallas_call(
        _qkv_kernel,
        grid=(ni, nj),
        in_specs=[
            pl.BlockSpec((tm, D), lambda i, j: (i, 0)),
            pl.BlockSpec((D, tn), lambda i, j: (0, j)),
            pl.BlockSpec((1, tn), lambda i, j: (0, j)),
            side_in,
        ],
        out_specs=[pl.BlockSpec((tm, tn), lambda i, j: (i, j)), side_out],
        out_shape=[jax.ShapeDtypeStruct((T, N), _bf16), side_shape],
        compiler_params=_params(2),
        name="qkv_proj",
    )(xn, w16, colscale, w_o)


def _bf16_round(x):
    bits = struct.unpack("<I", struct.pack("<f", x))[0]
    bits = (bits + 0x7FFF + ((bits >> 16) & 1)) & 0xFFFF0000
    return struct.unpack("<f", struct.pack("<I", bits))[0]


def _bf16_split(x, n):
    terms = []
    for _ in range(n):
        t = _bf16_round(x)
        terms.append(t)
        x -= t
    return terms


LOG2E = math.log2(math.e)
LOG2E_TERMS = _bf16_split(LOG2E, 4)
KPOS_RADIX_BITS = 6


def _attn_kernel(slopes_ref, q_ref, k_ref, v_ref, lq1_ref, lk1_ref, lq2_ref, lk2_ref, sg_ref, o_ref,
                 kaug_ref, qaug_ref, s0_ref, s1_ref, mrun_ref, m_ref, lrun_ref, l_ref, acc_ref,
                 *, tq, lambda_init, unroll):
    d = DIFF_HEAD_DIM
    S = q_ref.shape[0]
    nq = S // tq
    nt = len(LOG2E_TERMS)
    slope = slopes_ref[pl.program_id(1)]
    s_refs = (s0_ref, s1_ref)

    @pl.when((pl.program_id(0) == 0) & (pl.program_id(1) == 0))
    def _():
        pos = lax.broadcasted_iota(jnp.int32, (S, d), 0)
        lane = lax.broadcasted_iota(jnp.int32, (S, d), 1)
        lo = (pos & ((1 << KPOS_RADIX_BITS) - 1)).astype(_f32)
        hi = (pos >> KPOS_RADIX_BITS).astype(_f32) * float(1 << KPOS_RADIX_BITS)
        kaug_ref[...] = jnp.where(lane < nt, lo, jnp.where(lane < 2 * nt, hi, 0.0)).astype(_bf16)

    qlane = lax.broadcasted_iota(jnp.int32, (tq, d), 1)
    qaug = jnp.zeros((tq, d), _f32)
    for i, term in enumerate(LOG2E_TERMS):
        qaug = jnp.where((qlane == i) | (qlane == nt + i), slope * term, qaug)
    qaug_ref[...] = qaug.astype(_bf16)

    lam = (jnp.exp(jnp.sum(lq1_ref[...] * lk1_ref[...], keepdims=True))
           - jnp.exp(jnp.sum(lq2_ref[...] * lk2_ref[...], keepdims=True))
           + lambda_init)

    def rows(j):
        if isinstance(j, int):
            return pl.ds(j * tq, tq)
        return pl.ds(pl.multiple_of(j * tq, tq), tq)

    def lane_blocks(x):
        return [x[:, c0:c0 + LANES] for c0 in range(0, x.shape[1], LANES)]

    def scores_chunk(u, j):
        qi, comp = divmod(u, 2)
        cs = slice(comp * d, (comp + 1) * d)
        qa = jnp.concatenate([q_ref[rows(qi), cs], qaug_ref[...]], axis=1)
        ka = jnp.concatenate([k_ref[rows(j), cs], kaug_ref[rows(j), :]], axis=1)
        s = lax.dot_general(qa, ka, (((1,), (1,)), ((), ())), preferred_element_type=_f32)
        if isinstance(j, int) and j == qi:
            r = lax.broadcasted_iota(jnp.int32, (tq, tq), 0)
            c = lax.broadcasted_iota(jnp.int32, (tq, tq), 1)
            s = jnp.where(c <= r, s, -jnp.inf)
        s_refs[comp][j] = s
        mrun_ref[comp] = functools.reduce(jnp.maximum, lane_blocks(s), mrun_ref[comp])

    def probs_pv_chunk(u, j):
        qi, comp = divmod(u, 2)
        diagonal = isinstance(j, int) and j == qi
        p_rows = []
        for r0 in range(0, tq, ROW_GROUP):
            rs = slice(r0, r0 + ROW_GROUP)
            m = m_ref[comp, rs, :]
            lsum = lrun_ref[comp, rs, :]
            n_live = -(-(r0 + ROW_GROUP) // LANES) if diagonal else tq // LANES
            p_blocks = []
            for sb in lane_blocks(s_refs[comp][j, rs, 0:n_live * LANES]):
                pb = jnp.exp2(sb - m)
                lsum = lsum + pb
                p_blocks.append(pb.astype(_bf16))
            p_blocks += [jnp.zeros((ROW_GROUP, LANES), _bf16)] * (tq // LANES - n_live)
            lrun_ref[comp, rs, :] = lsum
            p_rows.append(jnp.concatenate(p_blocks, axis=1))
        p = jnp.concatenate(p_rows, axis=0)
        acc_ref[comp] += jnp.dot(p, v_ref[rows(j), :], preferred_element_type=_f32)

    def start_scores(u):
        mrun_ref[u % 2] = jnp.full((tq, LANES), -jnp.inf, _f32)

    def finish_scores(u):
        comp = u % 2
        m_ref[comp] = jnp.broadcast_to(jnp.max(mrun_ref[comp], axis=-1, keepdims=True), (tq, LANES))
        lrun_ref[comp] = jnp.zeros((tq, LANES), _f32)
        acc_ref[comp] = jnp.zeros((tq, 2 * d), _f32)

    def finish_probs(u):
        comp = u % 2
        l_ref[comp] = jnp.broadcast_to(jnp.sum(lrun_ref[comp], axis=-1, keepdims=True), (tq, LANES))

    def finish_tile(qi):
        gain = sg_ref[...] * (1.0 - lambda_init)
        for r0 in range(0, tq, ROW_GROUP):
            rs = slice(r0, r0 + ROW_GROUP)
            inv = [1.0 / l_ref[comp, rs, :] for comp in range(2)]
            inv = [jnp.concatenate([x] * (2 * d // LANES), axis=1) for x in inv]
            o = acc_ref[0, rs, :] * inv[0] - lam * (acc_ref[1, rs, :] * inv[1])
            ms = jnp.mean(o * o, axis=-1, keepdims=True)
            on = o * lax.rsqrt(ms + RMS_EPS) * gain
            o_ref[pl.ds(qi * tq + r0, ROW_GROUP), :] = on.astype(o_ref.dtype)

    n_units = 2 * nq
    chunks = lambda u: u // 2 + 1 if 0 <= u < n_units else 0

    start_scores(0)
    for j in range(chunks(0)):
        scores_chunk(0, j)
    finish_scores(0)
    for u in range(n_units):
        n_s, n_p = chunks(u + 1), chunks(u)
        if n_s:
            start_scores(u + 1)
        n_common = min(n_p - 1, n_s - 1) if n_s else n_p - 1

        def chunk_step(j, u=u, n_s=n_s):
            if n_s:
                scores_chunk(u + 1, j)
            probs_pv_chunk(u, j)

        def body(j, carry, chunk_step=chunk_step):
            chunk_step(j)
            return carry

        if n_common:
            lax.fori_loop(0, n_common, body, 0, unroll=min(unroll, n_common))
        for j in range(n_common, max(n_s, n_p)):
            if j < n_s:
                scores_chunk(u + 1, j)
            if j < n_p:
                probs_pv_chunk(u, j)
        finish_probs(u)
        if u % 2 == 1:
            finish_tile(u // 2)
        if n_s:
            finish_scores(u + 1)


def _diff_attention(qkv, slopes, lq1, lk1, lq2, lk2, subln_g, *, B, S, H, lambda_init, attn_layer, tq=512, unroll=3):
    d, e = DIFF_HEAD_DIM, V_HEAD_DIM
    kern = functools.partial(_attn_kernel, tq=tq, lambda_init=lambda_init, unroll=unroll)
    return pl.pallas_call(
        kern,
        grid_spec=pltpu.PrefetchScalarGridSpec(
            num_scalar_prefetch=1,
            grid=(B, H),
            in_specs=[
                pl.BlockSpec((S, e), lambda b, h, sl: (b, h)),
                pl.BlockSpec((S, e), lambda b, h, sl: (b, H + h)),
                pl.BlockSpec((S, e), lambda b, h, sl: (b, 2 * H + h)),
                pl.BlockSpec((None, 1, d), lambda b, h, sl: (attn_layer, 0, 0)),
                pl.BlockSpec((None, 1, d), lambda b, h, sl: (attn_layer, 0, 0)),
                pl.BlockSpec((None, 1, d), lambda b, h, sl: (attn_layer, 0, 0)),
                pl.BlockSpec((None, 1, d), lambda b, h, sl: (attn_layer, 0, 0)),
                pl.BlockSpec((None, 1, e), lambda b, h, sl: (attn_layer, 0, 0)),
            ],
            out_specs=pl.BlockSpec((S, e), lambda b, h, sl: (b, h)),
            scratch_shapes=[
                pltpu.VMEM((S, d), _bf16),
                pltpu.VMEM((tq, d), _bf16),
                pltpu.VMEM((S // tq, tq, tq), _f32),
                pltpu.VMEM((S // tq, tq, tq), _f32),
                pltpu.VMEM((2, tq, LANES), _f32),
                pltpu.VMEM((2, tq, LANES), _f32),
                pltpu.VMEM((2, tq, LANES), _f32),
                pltpu.VMEM((2, tq, LANES), _f32),
                pltpu.VMEM((2, tq, e), _f32),
            ],
        ),
        out_shape=jax.ShapeDtypeStruct((B * S, H * e), _bf16),
        compiler_params=_params(2),
        name="diff_attention",
    )(slopes, qkv, qkv, qkv, _rows(lq1), _rows(lk1), _rows(lq2), _rows(lk2), _rows(subln_g))


def _wo_kernel(o_ref, w_ref, res_ref, g_ref, h_ref, xn_ref):
    h = res_ref[...] + jnp.dot(o_ref[...], w_ref[...], preferred_element_type=_f32)
    h_ref[...] = h
    xn_ref[...] = _rms(h, g_ref[...]).astype(_bf16)


def _wo_proj(o, w16, res, norm_ffn, *, layer, tm=512):
    T, D = res.shape
    return pl.pallas_call(
        _wo_kernel,
        grid=(T // tm,),
        in_specs=[
            pl.BlockSpec((tm, D), lambda i: (i, 0)),
            pl.BlockSpec((D, D), lambda i: (0, 0), pipeline_mode=pl.Buffered(1)),
            pl.BlockSpec((tm, D), lambda i: (i, 0)),
            pl.BlockSpec((None, 1, D), lambda i: (layer, 0, 0)),
        ],
        out_specs=[
            pl.BlockSpec((tm, D), lambda i: (i, 0)),
            pl.BlockSpec((tm, D), lambda i: (i, 0)),
        ],
        out_shape=[
            jax.ShapeDtypeStruct((T, D), _f32),
            jax.ShapeDtypeStruct((T, D), _bf16),
        ],
        compiler_params=_params(1),
        name="wo_proj",
    )(o, w16, res, _rows(norm_ffn))


def kernel(x, norm_mix, norm_ffn, pool_w, pool_scale, w_qkv, lambda_q1, lambda_k1, lambda_q2, lambda_k2,
           subln_g, w_o, w_gate, w_up, w_down, final_norm):
    B, S, D = x.shape
    T = B * S
    H = D // V_HEAD_DIM

    h, xn = _pool_layer(x, norm_mix, pool_w, pool_scale, norm_ffn, layer=0, pool_layer=0)
    hmid, wd16, wqkv16 = _ffn_up(xn.reshape(T, D), w_gate, w_up, [(w_down, 0), (w_qkv, 0)], layer=0)
    h, xn = _ffn_down(hmid, wd16, h.reshape(T, D), norm_mix, 1, emit_residual=True)

    lambda_init = 0.8 - 0.6 * math.exp(-0.3 * 1)
    colscale = jnp.concatenate(
        [jnp.full((1, D), DIFF_HEAD_DIM ** -0.5 * LOG2E, _f32), jnp.ones((1, 2 * D), _f32)], axis=1)
    qkv, wo16 = _qkv_proj(xn, wqkv16, colscale, w_o, attn_layer=0)
    slope_values = [2.0 ** (-8.0 * (i + 1) / H) for i in range(H)]
    assert all(math.frexp(s)[0] == 0.5 for s in slope_values)
    slopes = jnp.asarray(slope_values, dtype=_f32)
    o = _diff_attention(qkv, slopes, lambda_q1, lambda_k1, lambda_q2, lambda_k2, subln_g,
                        B=B, S=S, H=H, lambda_init=lambda_init, attn_layer=0)
    h, xn = _wo_proj(o, wo16, h, norm_ffn, layer=1)
    hmid, wd16 = _ffn_up(xn, w_gate, w_up, [(w_down, 1)], layer=1)
    (out,) = _ffn_down(hmid, wd16, h, final_norm.reshape(1, D), 0, emit_residual=False)
    return out.reshape(B, S, D)
```

```python
import functools
import math
import struct

import jax
import jax.numpy as jnp
from jax import lax
from jax.experimental import pallas as pl
from jax.experimental.pallas import tpu as pltpu

RMS_EPS = 1e-6
POOL_WINDOWS = (2, 4, 8, 16)
POOL_HALO = 16
DIFF_HEAD_DIM = 128
V_HEAD_DIM = 2 * DIFF_HEAD_DIM
LANES = 128
ROW_GROUP = 64
FFN_UP_ROWS = 128
VMEM_LIMIT_BYTES = 127 * 512 * 1024

_f32 = jnp.float32
_bf16 = jnp.bfloat16


def _params(n_axes):
    return pltpu.CompilerParams(
        dimension_semantics=("arbitrary",) * n_axes,
        vmem_limit_bytes=VMEM_LIMIT_BYTES,
    )


def _rows(a):
    return a.reshape(a.shape[0], 1, a.shape[1])


def _rms(x, g):
    ms = jnp.mean(x * x, axis=-1, keepdims=True)
    return x * lax.rsqrt(ms + RMS_EPS) * g


def _pool_kernel(x_ref, halo_ref, gmix_ref, w_ref, scale_ref, gffn_ref, h_ref, xn_ref, *, ts, group):
    s = pl.program_id(1)
    x = x_ref[0]
    g = gmix_ref[...]
    hn = _rms(x, g)
    halo = jnp.where(s > 0, _rms(halo_ref[0], g), 0.0)
    ext = jnp.concatenate([halo, hn], axis=0)
    t = s * ts + lax.broadcasted_iota(jnp.int32, (ts, 1), 0)
    ssq = jnp.zeros((ts, 1), _f32)
    for gi, w in enumerate(POOL_WINDOWS):
        cols = slice(gi * group, (gi + 1) * group)
        acc = ext[:, cols]
        span = 1
        while span < w:
            acc = acc + pltpu.roll(acc, span, axis=0)
            span *= 2
        cnt = jnp.minimum(t + 1, w).astype(_f32)
        pooled = acc[POOL_HALO:, :] / cnt
        diff = (pooled - hn[:, cols]).astype(_bf16)
        y = jnp.dot(diff, w_ref[gi].astype(_bf16), preferred_element_type=_f32)
        hg = x[:, cols] + y * scale_ref[:, cols]
        h_ref[0, :, cols] = hg
        ssq = ssq + jnp.sum(hg * hg, axis=-1, keepdims=True)
    d_model = x.shape[-1]
    inv = lax.rsqrt(ssq / d_model + RMS_EPS)
    xn_ref[0] = (h_ref[0] * inv * gffn_ref[...]).astype(_bf16)


def _pool_layer(x, norm_mix, pool_w, pool_scale, norm_ffn, *, layer, pool_layer, ts=1024):
    B, S, D = x.shape
    _, G, C, _ = pool_w.shape
    halo_blocks = ts // POOL_HALO
    kern = functools.partial(_pool_kernel, ts=ts, group=C)
    return pl.pallas_call(
        kern,
        grid=(B, S // ts),
        in_specs=[
            pl.BlockSpec((1, ts, D), lambda b, s: (b, s, 0)),
            pl.BlockSpec((1, POOL_HALO, D), lambda b, s: (b, jnp.maximum(s * halo_blocks - 1, 0), 0)),
            pl.BlockSpec((None, 1, D), lambda b, s: (layer, 0, 0)),
            pl.BlockSpec((None, G, C, C), lambda b, s: (pool_layer, 0, 0, 0)),
            pl.BlockSpec((None, 1, D), lambda b, s: (pool_layer, 0, 0)),
            pl.BlockSpec((None, 1, D), lambda b, s: (layer, 0, 0)),
        ],
        out_specs=[
            pl.BlockSpec((1, ts, D), lambda b, s: (b, s, 0)),
            pl.BlockSpec((1, ts, D), lambda b, s: (b, s, 0)),
        ],
        out_shape=[
            jax.ShapeDtypeStruct((B, S, D), _f32),
            jax.ShapeDtypeStruct((B, S, D), _bf16),
        ],
        compiler_params=_params(2),
        name="pool_mixer",
    )(x, x, _rows(norm_mix), pool_w, _rows(pool_scale), _rows(norm_ffn))


def _ffn_up_kernel(xn_ref, wg_ref, wu_ref, *refs, n_side):
    side_in, o_ref, side_out = refs[:n_side], refs[n_side], refs[n_side + 1:]
    wg = wg_ref[...].astype(_bf16)
    wu = wu_ref[...].astype(_bf16)
    for r0 in range(0, xn_ref.shape[0], FFN_UP_ROWS):
        rs = slice(r0, r0 + FFN_UP_ROWS)
        xn = xn_ref[rs, :]
        gate = jnp.dot(xn, wg, preferred_element_type=_f32)
        up = jnp.dot(xn, wu, preferred_element_type=_f32)
        o_ref[rs, :] = (gate / (1.0 + jnp.exp(-gate)) * up).astype(o_ref.dtype)
    for src, dst in zip(side_in, side_out):
        dst[...] = src[...].astype(_bf16)


def _side_cast_specs(w, layer, n_steps, step_of):
    _, R, C = w.shape
    bf16_rows = 16
    slab = next(s for s in range(bf16_rows, R + 1, bf16_rows) if R % s == 0 and R // s <= n_steps)
    last = R // slab - 1
    return (pl.BlockSpec((None, slab, C), lambda *g: (layer, jnp.minimum(step_of(*g), last), 0)),
            pl.BlockSpec((slab, C), lambda *g: (jnp.minimum(step_of(*g), last), 0)),
            jax.ShapeDtypeStruct((R, C), _bf16))


def _ffn_up(xn, w_gate, w_up, side_weights, *, layer, tm=2048, tn=512):
    T, D = xn.shape
    F = w_gate.shape[2]
    ni, nj = T // tm, F // tn
    side = [_side_cast_specs(w, l, ni * nj, lambda i, j: i * nj + j) for w, l in side_weights]
    return pl.pallas_call(
        functools.partial(_ffn_up_kernel, n_side=len(side)),
        grid=(ni, nj),
        in_specs=[
            pl.BlockSpec((tm, D), lambda i, j: (i, 0)),
            pl.BlockSpec((None, D, tn), lambda i, j: (layer, 0, j)),
            pl.BlockSpec((None, D, tn), lambda i, j: (layer, 0, j)),
        ] + [s[0] for s in side],
        out_specs=[pl.BlockSpec((tm, tn), lambda i, j: (i, j))] + [s[1] for s in side],
        out_shape=[jax.ShapeDtypeStruct((T, F), _bf16)] + [s[2] for s in side],
        compiler_params=_params(2),
        name="ffn_up",
    )(xn, w_gate, w_up, *[w for w, _ in side_weights])


def _ffn_down_kernel(hm_ref, wd_hbm, res_ref, g_ref, *rest, emit_residual, n_chunks):
    outs, wd_ref, sems = rest[:-2], rest[-2], rest[-1]
    chunk = wd_ref.shape[0] // n_chunks

    def chunk_copy(c):
        rows = pl.ds(c * chunk, chunk)
        return pltpu.make_async_copy(wd_hbm.at[rows, :], wd_ref.at[rows, :], sems.at[c])

    def finish(h):
        if emit_residual:
            outs[0][...] = h
            outs[1][...] = _rms(h, g_ref[...]).astype(_bf16)
        else:
            outs[0][...] = _rms(h, g_ref[...])

    @pl.when(pl.program_id(0) == 0)
    def _():
        for c in range(n_chunks):
            chunk_copy(c).start()
        acc_ref = outs[0]
        acc_ref[...] = res_ref[...]
        for c in range(n_chunks):
            chunk_copy(c).wait()
            cols = slice(c * chunk, (c + 1) * chunk)
            acc_ref[...] += jnp.dot(hm_ref[:, cols], wd_ref[cols, :], preferred_element_type=_f32)
        finish(acc_ref[...])

    @pl.when(pl.program_id(0) > 0)
    def _():
        finish(res_ref[...] + jnp.dot(hm_ref[...], wd_ref[...], preferred_element_type=_f32))


def _ffn_down(hmid, wd16, res, g, g_row, *, emit_residual, tm=512, wd_chunk_rows=512):
    T, F = hmid.shape
    D = wd16.shape[1]
    n_chunks = F // wd_chunk_rows
    assert n_chunks * wd_chunk_rows == F and wd_chunk_rows % LANES == 0
    row_spec = pl.BlockSpec((tm, D), lambda i: (i, 0))
    if emit_residual:
        out_specs = [row_spec, row_spec]
        out_shape = [jax.ShapeDtypeStruct((T, D), _f32), jax.ShapeDtypeStruct((T, D), _bf16)]
    else:
        out_specs = [row_spec]
        out_shape = [jax.ShapeDtypeStruct((T, D), _f32)]
    return pl.pallas_call(
        functools.partial(_ffn_down_kernel, emit_residual=emit_residual, n_chunks=n_chunks),
        grid=(T // tm,),
        in_specs=[
            pl.BlockSpec((tm, F), lambda i: (i, 0)),
            pl.BlockSpec(memory_space=pl.ANY),
            row_spec,
            pl.BlockSpec((None, 1, D), lambda i: (g_row, 0, 0)),
        ],
        out_specs=out_specs,
        out_shape=out_shape,
        scratch_shapes=[pltpu.VMEM((F, D), _bf16), pltpu.SemaphoreType.DMA((n_chunks,))],
        compiler_params=_params(1),
        name="ffn_down",
    )(hmid, wd16, res, _rows(g))


def _qkv_kernel(xn_ref, w_ref, cs_ref, wo_ref, o_ref, wo16_ref):
    acc = jnp.dot(xn_ref[...], w_ref[...], preferred_element_type=_f32)
    o_ref[...] = (acc * cs_ref[...]).astype(o_ref.dtype)
    wo16_ref[...] = wo_ref[...].astype(_bf16)


def _qkv_proj(xn, w16, colscale, w_o, *, attn_layer, tm=2048, tn=1536):
    T, D = xn.shape
    N = w16.shape[1]
    ni, nj = T // tm, N // tn
    side_in, side_out, side_shape = _side_cast_specs(w_o, attn_layer, ni * nj, lambda i, j: i * nj + j)
    return pl.pallas_call(
        _qkv_kernel,
        grid=(ni, nj),
        in_specs=[
            pl.BlockSpec((tm, D), lambda i, j: (i, 0)),
            pl.BlockSpec((D, tn), lambda i, j: (0, j)),
            pl.BlockSpec((1, tn), lambda i, j: (0, j)),
            side_in,
        ],
        out_specs=[pl.BlockSpec((tm, tn), lambda i, j: (i, j)), side_out],
        out_shape=[jax.ShapeDtypeStruct((T, N), _bf16), side_shape],
        compiler_params=_params(2),
        name="qkv_proj",
    )(xn, w16, colscale, w_o)


def _bf16_round(x):
    bits = struct.unpack("<I", struct.pack("<f", x))[0]
    bits = (bits + 0x7FFF + ((bits >> 16) & 1)) & 0xFFFF0000
    return struct.unpack("<f", struct.pack("<I", bits))[0]


def _bf16_split(x, n):
    terms = []
    for _ in range(n):
        t = _bf16_round(x)
        terms.append(t)
        x -= t
    return terms


LOG2E = math.log2(math.e)
LOG2E_TERMS = _bf16_split(LOG2E, 4)
KPOS_RADIX_BITS = 6


def _attn_kernel(slopes_ref, q_ref, k_ref, v_ref, lq1_ref, lk1_ref, lq2_ref, lk2_ref, sg_ref, o_ref,
                 kaug_ref, qaug_ref, s0_ref, s1_ref, mrun_ref, m_ref, lrun_ref, l_ref, acc_ref,
                 *, tq, lambda_init, unroll):
    d = DIFF_HEAD_DIM
    S = q_ref.shape[0]
    nq = S // tq
    nt = len(LOG2E_TERMS)
    slope = slopes_ref[pl.program_id(1)]
    s_refs = (s0_ref, s1_ref)

    @pl.when((pl.program_id(0) == 0) & (pl.program_id(1) == 0))
    def _():
        pos = lax.broadcasted_iota(jnp.int32, (S, d), 0)
        lane = lax.broadcasted_iota(jnp.int32, (S, d), 1)
        lo = (pos & ((1 << KPOS_RADIX_BITS) - 1)).astype(_f32)
        hi = (pos >> KPOS_RADIX_BITS).astype(_f32) * float(1 << KPOS_RADIX_BITS)
        kaug_ref[...] = jnp.where(lane < nt, lo, jnp.where(lane < 2 * nt, hi, 0.0)).astype(_bf16)

    qlane = lax.broadcasted_iota(jnp.int32, (tq, d), 1)
    qaug = jnp.zeros((tq, d), _f32)
    for i, term in enumerate(LOG2E_TERMS):
        qaug = jnp.where((qlane == i) | (qlane == nt + i), slope * term, qaug)
    qaug_ref[...] = qaug.astype(_bf16)

    lam = (jnp.exp(jnp.sum(lq1_ref[...] * lk1_ref[...], keepdims=True))
           - jnp.exp(jnp.sum(lq2_ref[...] * lk2_ref[...], keepdims=True))
           + lambda_init)

    def rows(j):
        if isinstance(j, int):
            return pl.ds(j * tq, tq)
        return pl.ds(pl.multiple_of(j * tq, tq), tq)

    def lane_blocks(x):
        return [x[:, c0:c0 + LANES] for c0 in range(0, x.shape[1], LANES)]

    def scores_chunk(u, j):
        qi, comp = divmod(u, 2)
        cs = slice(comp * d, (comp + 1) * d)
        qa = jnp.concatenate([q_ref[rows(qi), cs], qaug_ref[...]], axis=1)
        ka = jnp.concatenate([k_ref[rows(j), cs], kaug_ref[rows(j), :]], axis=1)
        s = lax.dot_general(qa, ka, (((1,), (1,)), ((), ())), preferred_element_type=_f32)
        if isinstance(j, int) and j == qi:
            r = lax.broadcasted_iota(jnp.int32, (tq, tq), 0)
            c = lax.broadcasted_iota(jnp.int32, (tq, tq), 1)
            s = jnp.where(c <= r, s, -jnp.inf)
        s_refs[comp][j] = s
        mrun_ref[comp] = functools.reduce(jnp.maximum, lane_blocks(s), mrun_ref[comp])

    def probs_pv_chunk(u, j):
        qi, comp = divmod(u, 2)
        diagonal = isinstance(j, int) and j == qi
        p_rows = []
        for r0 in range(0, tq, ROW_GROUP):
            rs = slice(r0, r0 + ROW_GROUP)
            m = m_ref[comp, rs, :]
            lsum = lrun_ref[comp, rs, :]
            n_live = -(-(r0 + ROW_GROUP) // LANES) if diagonal else tq // LANES
            p_blocks = []
            for sb in lane_blocks(s_refs[comp][j, rs, 0:n_live * LANES]):
                pb = jnp.exp2(sb - m)
                lsum = lsum + pb
                p_blocks.append(pb.astype(_bf16))
            p_blocks += [jnp.zeros((ROW_GROUP, LANES), _bf16)] * (tq // LANES - n_live)
            lrun_ref[comp, rs, :] = lsum
            p_rows.append(jnp.concatenate(p_blocks, axis=1))
        p = jnp.concatenate(p_rows, axis=0)
        acc_ref[comp] += jnp.dot(p, v_ref[rows(j), :], preferred_element_type=_f32)

    def start_scores(u):
        mrun_ref[u % 2] = jnp.full((tq, LANES), -jnp.inf, _f32)

    def finish_scores(u):
        comp = u % 2
        m_ref[comp] = jnp.broadcast_to(jnp.max(mrun_ref[comp], axis=-1, keepdims=True), (tq, LANES))
        lrun_ref[comp] = jnp.zeros((tq, LANES), _f32)
        acc_ref[comp] = jnp.zeros((tq, 2 * d), _f32)

    def finish_probs(u):
        comp = u % 2
        l_ref[comp] = jnp.broadcast_to(jnp.sum(lrun_ref[comp], axis=-1, keepdims=True), (tq, LANES))

    def finish_tile(qi):
        gain = sg_ref[...] * (1.0 - lambda_init)
        for r0 in range(0, tq, ROW_GROUP):
            rs = slice(r0, r0 + ROW_GROUP)
            inv = [1.0 / l_ref[comp, rs, :] for comp in range(2)]
            inv = [jnp.concatenate([x] * (2 * d // LANES), axis=1) for x in inv]
            o = acc_ref[0, rs, :] * inv[0] - lam * (acc_ref[1, rs, :] * inv[1])
            ms = jnp.mean(o * o, axis=-1, keepdims=True)
            on = o * lax.rsqrt(ms + RMS_EPS) * gain
            o_ref[pl.ds(qi * tq + r0, ROW_GROUP), :] = on.astype(o_ref.dtype)

    n_units = 2 * nq
    chunks = lambda u: u // 2 + 1 if 0 <= u < n_units else 0

    start_scores(0)
    for j in range(chunks(0)):
        scores_chunk(0, j)
    finish_scores(0)
    for u in range(n_units):
        n_s, n_p = chunks(u + 1), chunks(u)
        if n_s:
            start_scores(u + 1)
        n_common = min(n_p - 1, n_s - 1) if n_s else n_p - 1

        def chunk_step(j, u=u, n_s=n_s):
            if n_s:
                scores_chunk(u + 1, j)
            probs_pv_chunk(u, j)

        def body(j, carry, chunk_step=chunk_step):
            chunk_step(j)
            return carry

        if n_common:
            lax.fori_loop(0, n_common, body, 0, unroll=min(unroll, n_common))
        for j in range(n_common, max(n_s, n_p)):
            if j < n_s:
                scores_chunk(u + 1, j)
            if j < n_p:
                probs_pv_chunk(u, j)
        finish_probs(u)
        if u % 2 == 1:
            finish_tile(u // 2)
        if n_s:
            finish_scores(u + 1)


def _diff_attention(qkv, slopes, lq1, lk1, lq2, lk2, subln_g, *, B, S, H, lambda_init, attn_layer, tq=512, unroll=3):
    d, e = DIFF_HEAD_DIM, V_HEAD_DIM
    kern = functools.partial(_attn_kernel, tq=tq, lambda_init=lambda_init, unroll=unroll)
    return pl.pallas_call(
        kern,
        grid_spec=pltpu.PrefetchScalarGridSpec(
            num_scalar_prefetch=1,
            grid=(B, H),
            in_specs=[
                pl.BlockSpec((S, e), lambda b, h, sl: (b, h)),
                pl.BlockSpec((S, e), lambda b, h, sl: (b, H + h)),
                pl.BlockSpec((S, e), lambda b, h, sl: (b, 2 * H + h)),
                pl.BlockSpec((None, 1, d), lambda b, h, sl: (attn_layer, 0, 0)),
                pl.BlockSpec((None, 1, d), lambda b, h, sl: (attn_layer, 0, 0)),
                pl.BlockSpec((None, 1, d), lambda b, h, sl: (attn_layer, 0, 0)),
                pl.BlockSpec((None, 1, d), lambda b, h, sl: (attn_layer, 0, 0)),
                pl.BlockSpec((None, 1, e), lambda b, h, sl: (attn_layer, 0, 0)),
            ],
            out_specs=pl.BlockSpec((S, e), lambda b, h, sl: (b, h)),
            scratch_shapes=[
                pltpu.VMEM((S, d), _bf16),
                pltpu.VMEM((tq, d), _bf16),
                pltpu.VMEM((S // tq, tq, tq), _f32),
                pltpu.VMEM((S // tq, tq, tq), _f32),
                pltpu.VMEM((2, tq, LANES), _f32),
                pltpu.VMEM((2, tq, LANES), _f32),
                pltpu.VMEM((2, tq, LANES), _f32),
                pltpu.VMEM((2, tq, LANES), _f32),
                pltpu.VMEM((2, tq, e), _f32),
            ],
        ),
        out_shape=jax.ShapeDtypeStruct((B * S, H * e), _bf16),
        compiler_params=_params(2),
        name="diff_attention",
    )(slopes, qkv, qkv, qkv, _rows(lq1), _rows(lk1), _rows(lq2), _rows(lk2), _rows(subln_g))


def _wo_kernel(o_ref, w_ref, res_ref, g_ref, h_ref, xn_ref):
    h = res_ref[...] + jnp.dot(o_ref[...], w_ref[...], preferred_element_type=_f32)
    h_ref[...] = h
    xn_ref[...] = _rms(h, g_ref[...]).astype(_bf16)


def _wo_proj(o, w16, res, norm_ffn, *, layer, tm=512):
    T, D = res.shape
    return pl.pallas_call(
        _wo_kernel,
        grid=(T // tm,),
        in_specs=[
            pl.BlockSpec((tm, D), lambda i: (i, 0)),
            pl.BlockSpec((D, D), lambda i: (0, 0), pipeline_mode=pl.Buffered(1)),
            pl.BlockSpec((tm, D), lambda i: (i, 0)),
            pl.BlockSpec((None, 1, D), lambda i: (layer, 0, 0)),
        ],
        out_specs=[
            pl.BlockSpec((tm, D), lambda i: (i, 0)),
            pl.BlockSpec((tm, D), lambda i: (i, 0)),
        ],
        out_shape=[
            jax.ShapeDtypeStruct((T, D), _f32),
            jax.ShapeDtypeStruct((T, D), _bf16),
        ],
        compiler_params=_params(1),
        name="wo_proj",
    )(o, w16, res, _rows(norm_ffn))


def kernel(x, norm_mix, norm_ffn, pool_w, pool_scale, w_qkv, lambda_q1, lambda_k1, lambda_q2, lambda_k2,
           subln_g, w_o, w_gate, w_up, w_down, final_norm):
    B, S, D = x.shape
    T = B * S
    H = D // V_HEAD_DIM

    h, xn = _pool_layer(x, norm_mix, pool_w, pool_scale, norm_ffn, layer=0, pool_layer=0)
    hmid, wd16, wqkv16 = _ffn_up(xn.reshape(T, D), w_gate, w_up, [(w_down, 0), (w_qkv, 0)], layer=0)
    h, xn = _ffn_down(hmid, wd16, h.reshape(T, D), norm_mix, 1, emit_residual=True)

    lambda_init = 0.8 - 0.6 * math.exp(-0.3 * 1)
    colscale = jnp.concatenate(
        [jnp.full((1, D), DIFF_HEAD_DIM ** -0.5 * LOG2E, _f32), jnp.ones((1, 2 * D), _f32)], axis=1)
    qkv, wo16 = _qkv_proj(xn, wqkv16, colscale, w_o, attn_layer=0)
    slope_values = [2.0 ** (-8.0 * (i + 1) / H) for i in range(H)]
    assert all(math.frexp(s)[0] == 0.5 for s in slope_values)
    slopes = jnp.asarray(slope_values, dtype=_f32)
    o = _diff_attention(qkv, slopes, lambda_q1, lambda_k1, lambda_q2, lambda_k2, subln_g,
                        B=B, S=S, H=H, lambda_init=lambda_init, attn_layer=0)
    h, xn = _wo_proj(o, wo16, h, norm_ffn, layer=1)
    hmid, wd16 = _ffn_up(xn, w_gate, w_up, [(w_down, 1)], layer=1)
    (out,) = _ffn_down(hmid, wd16, h, final_norm.reshape(1, D), 0, emit_residual=False)
    return out.reshape(B, S, D)
```

```python
import functools
import math
import struct

import jax
import jax.numpy as jnp
from jax import lax
from jax.experimental import pallas as pl
from jax.experimental.pallas import tpu as pltpu

RMS_EPS = 1e-6
POOL_WINDOWS = (2, 4, 8, 16)
POOL_HALO = 16
DIFF_HEAD_DIM = 128
V_HEAD_DIM = 2 * DIFF_HEAD_DIM
LANES = 128
ROW_GROUP = 64
FFN_UP_ROWS = 128
VMEM_LIMIT_BYTES = 127 * 512 * 1024

_f32 = jnp.float32
_bf16 = jnp.bfloat16


def _params(n_axes):
    return pltpu.CompilerParams(
        dimension_semantics=("arbitrary",) * n_axes,
        vmem_limit_bytes=VMEM_LIMIT_BYTES,
    )


def _rows(a):
    return a.reshape(a.shape[0], 1, a.shape[1])


def _rms(x, g):
    ms = jnp.mean(x * x, axis=-1, keepdims=True)
    return x * lax.rsqrt(ms + RMS_EPS) * g


def _pool_kernel(x_ref, halo_ref, gmix_ref, w_ref, scale_ref, gffn_ref, h_ref, xn_ref, *, ts, group):
    s = pl.program_id(1)
    x = x_ref[0]
    g = gmix_ref[...]
    hn = _rms(x, g)
    halo = jnp.where(s > 0, _rms(halo_ref[0], g), 0.0)
    ext = jnp.concatenate([halo, hn], axis=0)
    t = s * ts + lax.broadcasted_iota(jnp.int32, (ts, 1), 0)
    ssq = jnp.zeros((ts, 1), _f32)
    for gi, w in enumerate(POOL_WINDOWS):
        cols = slice(gi * group, (gi + 1) * group)
        acc = ext[:, cols]
        span = 1
        while span < w:
            acc = acc + pltpu.roll(acc, span, axis=0)
            span *= 2
        cnt = jnp.minimum(t + 1, w).astype(_f32)
        pooled = acc[POOL_HALO:, :] / cnt
        diff = (pooled - hn[:, cols]).astype(_bf16)
        y = jnp.dot(diff, w_ref[gi].astype(_bf16), preferred_element_type=_f32)
        hg = x[:, cols] + y * scale_ref[:, cols]
        h_ref[0, :, cols] = hg
        ssq = ssq + jnp.sum(hg * hg, axis=-1, keepdims=True)
    d_model = x.shape[-1]
    inv = lax.rsqrt(ssq / d_model + RMS_EPS)
    xn_ref[0] = (h_ref[0] * inv * gffn_ref[...]).astype(_bf16)


def _pool_layer(x, norm_mix, pool_w, pool_scale, norm_ffn, *, layer, pool_layer, ts=1024):
    B, S, D = x.shape
    _, G, C, _ = pool_w.shape
    halo_blocks = ts // POOL_HALO
    kern = functools.partial(_pool_kernel, ts=ts, group=C)
    return pl.pallas_call(
        kern,
        grid=(B, S // ts),
        in_specs=[
            pl.BlockSpec((1, ts, D), lambda b, s: (b, s, 0)),
            pl.BlockSpec((1, POOL_HALO, D), lambda b, s: (b, jnp.maximum(s * halo_blocks - 1, 0), 0)),
            pl.BlockSpec((None, 1, D), lambda b, s: (layer, 0, 0)),
            pl.BlockSpec((None, G, C, C), lambda b, s: (pool_layer, 0, 0, 0)),
            pl.BlockSpec((None, 1, D), lambda b, s: (pool_layer, 0, 0)),
            pl.BlockSpec((None, 1, D), lambda b, s: (layer, 0, 0)),
        ],
        out_specs=[
            pl.BlockSpec((1, ts, D), lambda b, s: (b, s, 0)),
            pl.BlockSpec((1, ts, D), lambda b, s: (b, s, 0)),
        ],
        out_shape=[
            jax.ShapeDtypeStruct((B, S, D), _f32),
            jax.ShapeDtypeStruct((B, S, D), _bf16),
        ],
        compiler_params=_params(2),
        name="pool_mixer",
    )(x, x, _rows(norm_mix), pool_w, _rows(pool_scale), _rows(norm_ffn))


def _ffn_up_kernel(xn_ref, wg_ref, wu_ref, *refs, n_side):
    side_in, o_ref, side_out = refs[:n_side], refs[n_side], refs[n_side + 1:]
    wg = wg_ref[...].astype(_bf16)
    wu = wu_ref[...].astype(_bf16)
    for r0 in range(0, xn_ref.shape[0], FFN_UP_ROWS):
        rs = slice(r0, r0 + FFN_UP_ROWS)
        xn = xn_ref[rs, :]
        gate = jnp.dot(xn, wg, preferred_element_type=_f32)
        up = jnp.dot(xn, wu, preferred_element_type=_f32)
        o_ref[rs, :] = (gate / (1.0 + jnp.exp(-gate)) * up).astype(o_ref.dtype)
    for src, dst in zip(side_in, side_out):
        dst[...] = src[...].astype(_bf16)


def _side_cast_specs(w, layer, n_steps, step_of):
    _, R, C = w.shape
    bf16_rows = 16
    slab = next(s for s in range(bf16_rows, R + 1, bf16_rows) if R % s == 0 and R // s <= n_steps)
    last = R // slab - 1
    return (pl.BlockSpec((None, slab, C), lambda *g: (layer, jnp.minimum(step_of(*g), last), 0)),
            pl.BlockSpec((slab, C), lambda *g: (jnp.minimum(step_of(*g), last), 0)),
            jax.ShapeDtypeStruct((R, C), _bf16))


def _ffn_up(xn, w_gate, w_up, side_weights, *, layer, tm=2048, tn=512):
    T, D = xn.shape
    F = w_gate.shape[2]
    ni, nj = T // tm, F // tn
    side = [_side_cast_specs(w, l, ni * nj, lambda i, j: i * nj + j) for w, l in side_weights]
    return pl.pallas_call(
        functools.partial(_ffn_up_kernel, n_side=len(side)),
        grid=(ni, nj),
        in_specs=[
            pl.BlockSpec((tm, D), lambda i, j: (i, 0)),
            pl.BlockSpec((None, D, tn), lambda i, j: (layer, 0, j)),
            pl.BlockSpec((None, D, tn), lambda i, j: (layer, 0, j)),
        ] + [s[0] for s in side],
        out_specs=[pl.BlockSpec((tm, tn), lambda i, j: (i, j))] + [s[1] for s in side],
        out_shape=[jax.ShapeDtypeStruct((T, F), _bf16)] + [s[2] for s in side],
        compiler_params=_params(2),
        name="ffn_up",
    )(xn, w_gate, w_up, *[w for w, _ in side_weights])


def _ffn_down_kernel(hm_ref, wd_hbm, res_ref, g_ref, *rest, emit_residual, n_chunks):
    outs, wd_ref, sems = rest[:-2], rest[-2], rest[-1]
    chunk = wd_ref.shape[0] // n_chunks

    def chunk_copy(c):
        rows = pl.ds(c * chunk, chunk)
        return pltpu.make_async_copy(wd_hbm.at[rows, :], wd_ref.at[rows, :], sems.at[c])

    def finish(h):
        if emit_residual:
            outs[0][...] = h
            outs[1][...] = _rms(h, g_ref[...]).astype(_bf16)
        else:
            outs[0][...] = _rms(h, g_ref[...])

    @pl.when(pl.program_id(0) == 0)
    def _():
        for c in range(n_chunks):
            chunk_copy(c).start()
        acc_ref = outs[0]
        acc_ref[...] = res_ref[...]
        for c in range(n_chunks):
            chunk_copy(c).wait()
            cols = slice(c * chunk, (c + 1) * chunk)
            acc_ref[...] += jnp.dot(hm_ref[:, cols], wd_ref[cols, :], preferred_element_type=_f32)
        finish(acc_ref[...])

    @pl.when(pl.program_id(0) > 0)
    def _():
        finish(res_ref[...] + jnp.dot(hm_ref[...], wd_ref[...], preferred_element_type=_f32))


def _ffn_down(hmid, wd16, res, g, g_row, *, emit_residual, tm=512, wd_chunk_rows=512):
    T, F = hmid.shape
    D = wd16.shape[1]
    n_chunks = F // wd_chunk_rows
    assert n_chunks * wd_chunk_rows == F and wd_chunk_rows % LANES == 0
    row_spec = pl.BlockSpec((tm, D), lambda i: (i, 0))
    if emit_residual:
        out_specs = [row_spec, row_spec]
        out_shape = [jax.ShapeDtypeStruct((T, D), _f32), jax.ShapeDtypeStruct((T, D), _bf16)]
    else:
        out_specs = [row_spec]
        out_shape = [jax.ShapeDtypeStruct((T, D), _f32)]
    return pl.pallas_call(
        functools.partial(_ffn_down_kernel, emit_residual=emit_residual, n_chunks=n_chunks),
        grid=(T // tm,),
        in_specs=[
            pl.BlockSpec((tm, F), lambda i: (i, 0)),
            pl.BlockSpec(memory_space=pl.ANY),
            row_spec,
            pl.BlockSpec((None, 1, D), lambda i: (g_row, 0, 0)),
        ],
        out_specs=out_specs,
        out_shape=out_shape,
        scratch_shapes=[pltpu.VMEM((F, D), _bf16), pltpu.SemaphoreType.DMA((n_chunks,))],
        compiler_params=_params(1),
        name="ffn_down",
    )(hmid, wd16, res, _rows(g))


def _qkv_kernel(xn_ref, w_ref, cs_ref, wo_ref, o_ref, wo16_ref):
    acc = jnp.dot(xn_ref[...], w_ref[...], preferred_element_type=_f32)
    o_ref[...] = (acc * cs_ref[...]).astype(o_ref.dtype)
    wo16_ref[...] = wo_ref[...].astype(_bf16)


def _qkv_proj(xn, w16, colscale, w_o, *, attn_layer, tm=2048, tn=1536):
    T, D = xn.shape
    N = w16.shape[1]
    ni, nj = T // tm, N // tn
    side_in, side_out, side_shape = _side_cast_specs(w_o, attn_layer, ni * nj, lambda i, j: i * nj + j)
    return pl.pallas_call(
        _qkv_kernel,
        grid=(ni, nj),
        in_specs=[
            pl.BlockSpec((tm, D), lambda i, j: (i, 0)),
            pl.BlockSpec((D, tn), lambda i, j: (0, j)),
            pl.BlockSpec((1, tn), lambda i, j: (0, j)),
            side_in,
        ],
        out_specs=[pl.BlockSpec((tm, tn), lambda i, j: (i, j)), side_out],
        out_shape=[jax.ShapeDtypeStruct((T, N), _bf16), side_shape],
        compiler_params=_params(2),
        name="qkv_proj",
    )(xn, w16, colscale, w_o)


def _bf16_round(x):
    bits = struct.unpack("<I", struct.pack("<f", x))[0]
    bits = (bits + 0x7FFF + ((bits >> 16) & 1)) & 0xFFFF0000
    return struct.unpack("<f", struct.pack("<I", bits))[0]


def _bf16_split(x, n):
    terms = []
    for _ in range(n):
        t = _bf16_round(x)
        terms.append(t)
        x -= t
    return terms


LOG2E = math.log2(math.e)
LOG2E_TERMS = _bf16_split(LOG2E, 4)
KPOS_RADIX_BITS = 6


def _attn_kernel(slopes_ref, q_ref, k_ref, v_ref, lq1_ref, lk1_ref, lq2_ref, lk2_ref, sg_ref, o_ref,
                 kaug_ref, qaug_ref, s0_ref, s1_ref, mrun_ref, lrun_ref, acc_ref,
                 *, tq, lambda_init, unroll):
    d = DIFF_HEAD_DIM
    S = q_ref.shape[0]
    nq = S // tq
    nt = len(LOG2E_TERMS)
    slope = slopes_ref[pl.program_id(1)]
    s_refs = (s0_ref, s1_ref)

    @pl.when((pl.program_id(0) == 0) & (pl.program_id(1) == 0))
    def _():
        pos = lax.broadcasted_iota(jnp.int32, (S, d), 0)
        lane = lax.broadcasted_iota(jnp.int32, (S, d), 1)
        lo = (pos & ((1 << KPOS_RADIX_BITS) - 1)).astype(_f32)
        hi = (pos >> KPOS_RADIX_BITS).astype(_f32) * float(1 << KPOS_RADIX_BITS)
        kaug_ref[...] = jnp.where(lane < nt, lo, jnp.where(lane < 2 * nt, hi, 0.0)).astype(_bf16)

    qlane = lax.broadcasted_iota(jnp.int32, (tq, d), 1)
    qaug = jnp.zeros((tq, d), _f32)
    for i, term in enumerate(LOG2E_TERMS):
        qaug = jnp.where((qlane == i) | (qlane == nt + i), slope * term, qaug)
    qaug_ref[...] = qaug.astype(_bf16)

    lam = (jnp.exp(jnp.sum(lq1_ref[...] * lk1_ref[...], keepdims=True))
           - jnp.exp(jnp.sum(lq2_ref[...] * lk2_ref[...], keepdims=True))
           + lambda_init)

    def rows(j):
        if isinstance(j, int):
            return pl.ds(j * tq, tq)
        return pl.ds(pl.multiple_of(j * tq, tq), tq)

    def lane_blocks(x):
        return [x[:, c0:c0 + LANES] for c0 in range(0, x.shape[1], LANES)]

    def scores_chunk(u, j):
        qi, comp = divmod(u, 2)
        cs = slice(comp * d, (comp + 1) * d)
        if isinstance(j, int) and j == qi:
            scores_diagonal(qi, comp, cs)
            return
        qa = jnp.concatenate([q_ref[rows(qi), cs], qaug_ref[...]], axis=1)
        ka = jnp.concatenate([k_ref[rows(j), cs], kaug_ref[rows(j), :]], axis=1)
        s = lax.dot_general(qa, ka, (((1,), (1,)), ((), ())), preferred_element_type=_f32)
        s_refs[comp][j] = s
        mrun_ref[comp] = functools.reduce(jnp.maximum, lane_blocks(s), mrun_ref[comp])

    def scores_diagonal(qi, comp, cs):
        half = tq // 2
        for part, n_keys in ((0, half), (1, tq)):
            qrows = pl.ds(qi * tq + part * half, half)
            krows = pl.ds(qi * tq, n_keys)
            qa = jnp.concatenate([q_ref[qrows, cs], qaug_ref[0:half, :]], axis=1)
            ka = jnp.concatenate([k_ref[krows, cs], kaug_ref[krows, :]], axis=1)
            s = lax.dot_general(qa, ka, (((1,), (1,)), ((), ())), preferred_element_type=_f32)
            r = lax.broadcasted_iota(jnp.int32, (half, n_keys), 0) + part * half
            c = lax.broadcasted_iota(jnp.int32, (half, n_keys), 1)
            s = jnp.where(c <= r, s, -jnp.inf)
            prs = slice(part * half, (part + 1) * half)
            s_refs[comp][qi, prs, 0:n_keys] = s
            mrun_ref[comp, prs, :] = functools.reduce(jnp.maximum, lane_blocks(s), mrun_ref[comp, prs, :])

    def probs_pv_chunk(u, j):
        qi, comp = divmod(u, 2)
        diagonal = isinstance(j, int) and j == qi
        p_rows = []
        for r0 in range(0, tq, ROW_GROUP):
            rs = slice(r0, r0 + ROW_GROUP)
            m = mrun_ref[comp, rs, :]
            lsum = lrun_ref[comp, rs, :]
            n_live = -(-(r0 + ROW_GROUP) // LANES) if diagonal else tq // LANES
            p_blocks = []
            for sb in lane_blocks(s_refs[comp][j, rs, 0:n_live * LANES]):
                pb = jnp.exp2(sb - m)
                lsum = lsum + pb
                p_blocks.append(pb.astype(_bf16))
            p_blocks += [jnp.zeros((ROW_GROUP, LANES), _bf16)] * (tq // LANES - n_live)
            lrun_ref[comp, rs, :] = lsum
            p_rows.append(jnp.concatenate(p_blocks, axis=1))
        if diagonal:
            half = tq // 2
            n_grp = half // ROW_GROUP
            p_top = jnp.concatenate([pr[:, 0:half] for pr in p_rows[:n_grp]], axis=0)
            p_bot = jnp.concatenate(p_rows[n_grp:], axis=0)
            acc_ref[comp, 0:half, :] += jnp.dot(p_top, v_ref[pl.ds(j * tq, half), :], preferred_element_type=_f32)
            acc_ref[comp, half:tq, :] += jnp.dot(p_bot, v_ref[rows(j), :], preferred_element_type=_f32)
        else:
            p = jnp.concatenate(p_rows, axis=0)
            acc_ref[comp] += jnp.dot(p, v_ref[rows(j), :], preferred_element_type=_f32)

    def start_scores(u):
        mrun_ref[u % 2] = jnp.full((tq, LANES), -jnp.inf, _f32)

    def finish_scores(u):
        comp = u % 2
        mrun_ref[comp] = jnp.broadcast_to(jnp.max(mrun_ref[comp], axis=-1, keepdims=True), (tq, LANES))
        lrun_ref[comp] = jnp.zeros((tq, LANES), _f32)
        acc_ref[comp] = jnp.zeros((tq, 2 * d), _f32)

    def finish_probs(u):
        comp = u % 2
        lrun_ref[comp] = jnp.broadcast_to(jnp.sum(lrun_ref[comp], axis=-1, keepdims=True), (tq, LANES))

    def finish_tile(qi):
        gain = sg_ref[...] * (1.0 - lambda_init)
        for r0 in range(0, tq, ROW_GROUP):
            rs = slice(r0, r0 + ROW_GROUP)
            inv = [1.0 / lrun_ref[comp, rs, :] for comp in range(2)]
            inv = [jnp.concatenate([x] * (2 * d // LANES), axis=1) for x in inv]
            o = acc_ref[0, rs, :] * inv[0] - lam * (acc_ref[1, rs, :] * inv[1])
            ms = jnp.mean(o * o, axis=-1, keepdims=True)
            on = o * lax.rsqrt(ms + RMS_EPS) * gain
            o_ref[pl.ds(qi * tq + r0, ROW_GROUP), :] = on.astype(o_ref.dtype)

    n_units = 2 * nq
    chunks = lambda u: u // 2 + 1 if 0 <= u < n_units else 0

    start_scores(0)
    for j in range(chunks(0)):
        scores_chunk(0, j)
    finish_scores(0)
    for u in range(n_units):
        n_s, n_p = chunks(u + 1), chunks(u)
        if n_s:
            start_scores(u + 1)
        n_common = min(n_p - 1, n_s - 1) if n_s else n_p - 1

        def chunk_step(j, u=u, n_s=n_s):
            if n_s:
                scores_chunk(u + 1, j)
            probs_pv_chunk(u, j)

        def body(j, carry, chunk_step=chunk_step):
            chunk_step(j)
            return carry

        if n_common:
            lax.fori_loop(0, n_common, body, 0, unroll=min(unroll, n_common))
        for j in range(n_common, max(n_s, n_p)):
            if j < n_s:
                scores_chunk(u + 1, j)
            if j < n_p:
                probs_pv_chunk(u, j)
        finish_probs(u)
        if u % 2 == 1:
            finish_tile(u // 2)
        if n_s:
            finish_scores(u + 1)


def _diff_attention(qkv, slopes, lq1, lk1, lq2, lk2, subln_g, *, B, S, H, lambda_init, attn_layer, tq=512, unroll=3):
    d, e = DIFF_HEAD_DIM, V_HEAD_DIM
    kern = functools.partial(_attn_kernel, tq=tq, lambda_init=lambda_init, unroll=unroll)
    return pl.pallas_call(
        kern,
        grid_spec=pltpu.PrefetchScalarGridSpec(
            num_scalar_prefetch=1,
            grid=(B, H),
            in_specs=[
                pl.BlockSpec((S, e), lambda b, h, sl: (b, h)),
                pl.BlockSpec((S, e), lambda b, h, sl: (b, H + h)),
                pl.BlockSpec((S, e), lambda b, h, sl: (b, 2 * H + h)),
                pl.BlockSpec((None, 1, d), lambda b, h, sl: (attn_layer, 0, 0)),
                pl.BlockSpec((None, 1, d), lambda b, h, sl: (attn_layer, 0, 0)),
                pl.BlockSpec((None, 1, d), lambda b, h, sl: (attn_layer, 0, 0)),
                pl.BlockSpec((None, 1, d), lambda b, h, sl: (attn_layer, 0, 0)),
                pl.BlockSpec((None, 1, e), lambda b, h, sl: (attn_layer, 0, 0)),
            ],
            out_specs=pl.BlockSpec((S, e), lambda b, h, sl: (b, h)),
            scratch_shapes=[
                pltpu.VMEM((S, d), _bf16),
                pltpu.VMEM((tq, d), _bf16),
                pltpu.VMEM((S // tq, tq, tq), _f32),
                pltpu.VMEM((S // tq, tq, tq), _f32),
                pltpu.VMEM((2, tq, LANES), _f32),
                pltpu.VMEM((2, tq, LANES), _f32),
                pltpu.VMEM((2, tq, e), _f32),
            ],
        ),
        out_shape=jax.ShapeDtypeStruct((B * S, H * e), _bf16),
        compiler_params=_params(2),
        name="diff_attention",
    )(slopes, qkv, qkv, qkv, _rows(lq1), _rows(lk1), _rows(lq2), _rows(lk2), _rows(subln_g))


def _wo_kernel(o_ref, w_ref, res_ref, g_ref, h_ref, xn_ref):
    h = res_ref[...] + jnp.dot(o_ref[...], w_ref[...], preferred_element_type=_f32)
    h_ref[...] = h
    xn_ref[...] = _rms(h, g_ref[...]).astype(_bf16)


def _wo_proj(o, w16, res, norm_ffn, *, layer, tm=1024):
    T, D = res.shape
    return pl.pallas_call(
        _wo_kernel,
        grid=(T // tm,),
        in_specs=[
            pl.BlockSpec((tm, D), lambda i: (i, 0)),
            pl.BlockSpec((D, D), lambda i: (0, 0), pipeline_mode=pl.Buffered(1)),
            pl.BlockSpec((tm, D), lambda i: (i, 0)),
            pl.BlockSpec((None, 1, D), lambda i: (layer, 0, 0)),
        ],
        out_specs=[
            pl.BlockSpec((tm, D), lambda i: (i, 0)),
            pl.BlockSpec((tm, D), lambda i: (i, 0)),
        ],
        out_shape=[
            jax.ShapeDtypeStruct((T, D), _f32),
            jax.ShapeDtypeStruct((T, D), _bf16),
        ],
        compiler_params=_params(1),
        name="wo_proj",
    )(o, w16, res, _rows(norm_ffn))


def kernel(x, norm_mix, norm_ffn, pool_w, pool_scale, w_qkv, lambda_q1, lambda_k1, lambda_q2, lambda_k2,
           subln_g, w_o, w_gate, w_up, w_down, final_norm):
    B, S, D = x.shape
    T = B * S
    H = D // V_HEAD_DIM

    h, xn = _pool_layer(x, norm_mix, pool_w, pool_scale, norm_ffn, layer=0, pool_layer=0)
    hmid, wd16, wqkv16 = _ffn_up(xn.reshape(T, D), w_gate, w_up, [(w_down, 0), (w_qkv, 0)], layer=0)
    h, xn = _ffn_down(hmid, wd16, h.reshape(T, D), norm_mix, 1, emit_residual=True)

    lambda_init = 0.8 - 0.6 * math.exp(-0.3 * 1)
    colscale = jnp.concatenate(
        [jnp.full((1, D), DIFF_HEAD_DIM ** -0.5 * LOG2E, _f32), jnp.ones((1, 2 * D), _f32)], axis=1)
    qkv, wo16 = _qkv_proj(xn, wqkv16, colscale, w_o, attn_layer=0)
    slope_values = [2.0 ** (-8.0 * (i + 1) / H) for i in range(H)]
    assert all(math.frexp(s)[0] == 0.5 for s in slope_values)
    slopes = jnp.asarray(slope_values, dtype=_f32)
    o = _diff_attention(qkv, slopes, lambda_q1, lambda_k1, lambda_q2, lambda_k2, subln_g,
                        B=B, S=S, H=H, lambda_init=lambda_init, attn_layer=0)
    h, xn = _wo_proj(o, wo16, h, norm_ffn, layer=1)
    hmid, wd16 = _ffn_up(xn, w_gate, w_up, [(w_down, 1)], layer=1)
    (out,) = _ffn_down(hmid, wd16, h, final_norm.reshape(1, D), 0, emit_residual=False)
    return out.reshape(B, S, D)
```
